```python
import jax
import jax.numpy as jnp
from jax import lax
import numpy as np

D_MODEL = 1024
BATCH = 8
SEQ = 2048
DEPTH = 2
DEC_BATCH = 1
DEC_SEQ = 16384
PAST_LEN = 128

GRID_W = 64
HEAD_DIM = 64
NORM_EPS = 1e-6
MLSTM_HEADS = 4
MLSTM_W = MLSTM_HEADS * HEAD_DIM
MLSTM_CHUNK = 128
CONV_K = 3
FGATE_BIAS = 3.0
GQA_Q_HEADS = 6
GQA_KV_HEADS = 2
GQA_W = GQA_Q_HEADS * HEAD_DIM
GQA_KV_W = GQA_KV_HEADS * HEAD_DIM
AXIS_DIM = HEAD_DIM // 2
ROPE_THETA = 10000.0
Q_BLOCK = 128
DIL_PATTERNS = ((128, 1), (512, 4), (2048, 16))
DIL_HEADS_PER_GROUP = 2
DIL_HEADS = DIL_HEADS_PER_GROUP * len(DIL_PATTERNS)
DIL_W = DIL_HEADS * HEAD_DIM
MIX_W = MLSTM_W + GQA_W + DIL_W
IN_SPLITS = (MLSTM_W, MLSTM_W, MLSTM_W, MLSTM_W, 4 * MLSTM_HEADS, MLSTM_W,
             GQA_W, GQA_KV_W, GQA_KV_W, GQA_W,
             DIL_W, DIL_W, DIL_W, DIL_W)
IN_COLS = sum(IN_SPLITS)

kernel_name = 'hybrid_mlstm_gqa_dilated_encoder'

F32 = jnp.float32


def rms_norm(x, g):
    xf = x.astype(F32)
    y = xf * lax.rsqrt(jnp.mean(xf * xf, axis=-1, keepdims=True) + NORM_EPS)
    return (y * g.astype(F32)).astype(x.dtype)


def head_rms(a, g):
    return a * lax.rsqrt(jnp.mean(a * a, axis=-1, keepdims=True) + NORM_EPS) * g.astype(F32)


def to_heads(a, n):
    B, T, _ = a.shape
    return a.reshape(B, T, n, -1).transpose(0, 2, 1, 3)


def from_heads(a):
    B, H, T, D = a.shape
    return a.transpose(0, 2, 1, 3).reshape(B, T, H * D)


def centred_dwconv(x, w, b):
    K = w.shape[0]
    p = K // 2
    T = x.shape[1]
    xp = jnp.pad(x, ((0, 0), (p, p), (0, 0)))
    out = xp[:, 0:T] * w[0].astype(F32)
    for j in range(1, K):
        out = out + xp[:, j:j + T] * w[j].astype(F32)
    return out + b.astype(F32)


def axial_rope(T):
    rows = T // GRID_W
    r, c = jnp.meshgrid(jnp.arange(rows), jnp.arange(GRID_W), indexing='ij')
    r = r.reshape(-1).astype(F32)
    c = c.reshape(-1).astype(F32)
    inv = ROPE_THETA ** (-jnp.arange(0, AXIS_DIM, 2, dtype=F32) / AXIS_DIM)
    ang = jnp.concatenate([r[:, None] * inv, c[:, None] * inv], axis=-1)
    return jnp.cos(ang), jnp.sin(ang)


def apply_rope(x, cos, sin):
    half = HEAD_DIM // 2
    x1, x2 = x[..., :half], x[..., half:]
    return jnp.concatenate([x1 * cos - x2 * sin, x1 * sin + x2 * cos], axis=-1)


def alibi_slopes():
    return jnp.exp2(-8.0 * jnp.arange(1, DIL_HEADS + 1, dtype=F32) / DIL_HEADS)


def mlstm_scan(q, k, v, i_pre, f_pre):
    B, H, T, Dh = q.shape
    L = MLSTM_CHUNK
    nc = T // L
    log_f = jax.nn.log_sigmoid(f_pre)

    def chunks(a):
        return jnp.moveaxis(a.reshape(B, H, nc, L, *a.shape[3:]), 2, 0)

    causal = jnp.tril(jnp.ones((L, L), dtype=bool))

    def step(carry, xs):
        C, n, m = carry
        qc, kc, vc, ic, fc = xs
        bcum = jnp.cumsum(fc, axis=-1)
        b_tot = bcum[..., -1]
        d_intra = jnp.where(causal, bcum[..., :, None] - bcum[..., None, :] + ic[..., None, :], -jnp.inf)
        d_inter = bcum + m[..., None]
        m_row = jnp.maximum(d_inter, jnp.max(d_intra, axis=-1))
        w_intra = jnp.exp(d_intra - m_row[..., None]) * jnp.einsum('bhld,bhsd->bhls', qc, kc)
        w_inter = jnp.exp(d_inter - m_row)
        num = (w_inter[..., None] * jnp.einsum('bhld,bhde->bhle', qc, C)
               + jnp.einsum('bhls,bhse->bhle', w_intra, vc))
        den = w_inter * jnp.einsum('bhld,bhd->bhl', qc, n) + jnp.sum(w_intra, axis=-1)
        h = num / jnp.maximum(jnp.abs(den), jnp.exp(-m_row))[..., None]
        g = b_tot[..., None] - bcum + ic
        m_new = jnp.maximum(b_tot + m, jnp.max(g, axis=-1))
        decay = jnp.exp(b_tot + m - m_new)
        kw = jnp.exp(g - m_new[..., None])[..., None] * kc
        C_new = decay[..., None, None] * C + jnp.einsum('bhsd,bhse->bhde', kw, vc)
        n_new = decay[..., None] * n + jnp.sum(kw, axis=2)
        return (C_new, n_new, m_new), h

    init = (jnp.zeros((B, H, Dh, Dh), F32), jnp.zeros((B, H, Dh), F32), jnp.zeros((B, H), F32))
    xs = (chunks(q), chunks(k), chunks(v), chunks(i_pre), chunks(log_f))
    _, hs = lax.scan(step, init, xs)
    return jnp.moveaxis(hs, 0, 2).reshape(B, H, T, Dh)


def mlstm_mixer(q, k, v, o, gates, z, conv_w, conv_b, out_gain):
    qk = jax.nn.silu(centred_dwconv(jnp.concatenate([q, k], axis=-1), conv_w, conv_b))
    q, k = jnp.split(qk, 2, axis=-1)
    qh = to_heads(q, MLSTM_HEADS)
    kh = to_heads(k, MLSTM_HEADS) * HEAD_DIM ** -0.5
    vh = to_heads(v, MLSTM_HEADS)
    i_f, f_f, i_b, f_b = jnp.split(jnp.swapaxes(gates, 1, 2), 4, axis=1)
    h_fwd = mlstm_scan(qh, kh, vh, i_f, f_f)
    rev = lambda t: jnp.flip(t, axis=2)
    h_bwd = rev(mlstm_scan(rev(qh), rev(kh), rev(vh), rev(i_b), rev(f_b)))
    h = jax.nn.sigmoid(to_heads(o, MLSTM_HEADS)) * (h_fwd + h_bwd)
    hn = h * lax.rsqrt(jnp.mean(h * h, axis=-1, keepdims=True) + NORM_EPS)
    return from_heads(hn) * out_gain.astype(F32) * jax.nn.silu(z)


def gqa_mixer(q, k, v, z, q_norm, k_norm, cos, sin):
    B, T, _ = q.shape
    G = GQA_Q_HEADS // GQA_KV_HEADS
    qh = apply_rope(head_rms(to_heads(q, GQA_Q_HEADS), q_norm), cos, sin) * HEAD_DIM ** -0.5
    kh = apply_rope(head_rms(to_heads(k, GQA_KV_HEADS), k_norm), cos, sin)
    vh = to_heads(v, GQA_KV_HEADS)
    nb = T // Q_BLOCK
    qb = jnp.moveaxis(qh.reshape(B, GQA_KV_HEADS, G, nb, Q_BLOCK, HEAD_DIM), 3, 0)

    def attend(qblk):
        s = jnp.einsum('bkgqd,bksd->bkgqs', qblk, kh)
        p = jax.nn.softmax(s, axis=-1)
        return jnp.einsum('bkgqs,bksd->bkgqd', p, vh)

    ob = lax.map(attend, qb)
    o = jnp.moveaxis(ob, 0, 3).reshape(B, GQA_Q_HEADS, T, HEAD_DIM)
    return from_heads(o) * jax.nn.silu(z)


def dilated_branch(q, k, v, slopes, dilation, n_side):
    B, H, T, Dh = q.shape
    S = T // dilation
    nblk = -(-S // n_side)
    pad = nblk * n_side - S

    def gather_blocks(a):
        a = jnp.swapaxes(a.reshape(B, H, S, dilation, Dh), 2, 3)
        a = jnp.pad(a, ((0, 0), (0, 0), (0, 0), (0, pad), (0, 0)))
        return a.reshape(B, H, dilation, nblk, n_side, Dh)

    qb, kb, vb = gather_blocks(q), gather_blocks(k), gather_blocks(v)

    def band(a):
        ap = jnp.pad(a, ((0, 0), (0, 0), (0, 0), (1, 1), (0, 0), (0, 0)))
        return jnp.concatenate([ap[:, :, :, :-2], ap[:, :, :, 1:-1], ap[:, :, :, 2:]], axis=4)

    kw, vw = band(kb), band(vb)
    qi = jnp.arange(n_side)[:, None]
    kj = jnp.arange(3 * n_side)[None, :]
    rel = kj - n_side - qi
    key_pos = jnp.arange(nblk)[:, None, None] * n_side + (kj - n_side)[None]
    valid = (jnp.abs(rel) <= n_side)[None] & (key_pos >= 0) & (key_pos < S)
    bias = -slopes[:, None, None, None, None] * (jnp.abs(rel) * dilation).astype(F32)
    s = jnp.einsum('bhrnqd,bhrnkd->bhrnqk', qb, kw) * HEAD_DIM ** -0.5 + bias
    s = jnp.where(valid, s, -jnp.inf)
    lse = jax.nn.logsumexp(s, axis=-1)
    o = jnp.einsum('bhrnqk,bhrnkd->bhrnqd', jnp.exp(s - lse[..., None]), vw)

    def scatter_back(a, rest):
        a = a.reshape(B, H, dilation, nblk * n_side, *rest)[:, :, :, :S]
        return jnp.swapaxes(a, 2, 3).reshape(B, H, T, *rest)

    return scatter_back(o, (Dh,)), scatter_back(lse, ())


def dilated_mixer(q, k, v, z, slopes):
    qh, kh, vh = to_heads(q, DIL_HEADS), to_heads(k, DIL_HEADS), to_heads(v, DIL_HEADS)
    outs, lses = [], []
    for g, (window, dilation) in enumerate(DIL_PATTERNS):
        sl = slice(g * DIL_HEADS_PER_GROUP, (g + 1) * DIL_HEADS_PER_GROUP)
        o_g, l_g = dilated_branch(qh[:, sl], kh[:, sl], vh[:, sl], slopes[sl], dilation, window // (2 * dilation))
        outs.append(o_g)
        lses.append(l_g)
    alpha = jax.nn.softmax(jnp.stack(lses, axis=0), axis=0)
    o = jnp.concatenate([outs[g] * alpha[g][..., None] for g in range(len(DIL_PATTERNS))], axis=1)
    return from_heads(o) * jax.nn.silu(z)


def mixer_layer(h, w_in, gate_bias, conv_w, conv_b, out_gain, q_norm, k_norm, w_out, cos, sin, slopes):
    proj = jnp.matmul(h, w_in).astype(F32)
    cuts = [int(c) for c in np.cumsum(IN_SPLITS)[:-1]]
    (mq, mk, mv, mo, mg, mz, gq, gk, gv, gz, dq, dk, dv, dz) = jnp.split(proj, cuts, axis=-1)
    a = mlstm_mixer(mq, mk, mv, mo, mg + gate_bias.astype(F32), mz, conv_w, conv_b, out_gain)
    b = gqa_mixer(gq, gk, gv, gz, q_norm, k_norm, cos, sin)
    c = dilated_mixer(dq, dk, dv, dz, slopes)
    mixed = jnp.concatenate([a, b, c], axis=-1).astype(w_out.dtype)
    return jnp.matmul(mixed, w_out)


def trunk(x, norm_pre, w_in, mlstm_gate_bias, mlstm_conv_w, mlstm_conv_b, mlstm_out_gain,
          gqa_q_norm, gqa_k_norm, w_out, norm_post):
    T = x.shape[1]
    cos, sin = axial_rope(T)
    slopes = alibi_slopes()
    for l in range(DEPTH):
        h = rms_norm(x, norm_pre[l])
        y = mixer_layer(h, w_in[l], mlstm_gate_bias[l], mlstm_conv_w[l], mlstm_conv_b[l], mlstm_out_gain[l],
                        gqa_q_norm[l], gqa_k_norm[l], w_out[l], cos, sin, slopes)
        x = x + rms_norm(y, norm_post[l]).astype(x.dtype)
    return x


def setup_inputs(seed: int = 0) -> dict:
    key = jax.random.key(seed)
    ks = jax.random.split(key, 12)
    nrm = lambda k, s: jax.random.normal(k, s, dtype=F32)
    gate_mask = jnp.asarray(np.repeat(np.array([0.0, 1.0, 0.0, 1.0], np.float32), MLSTM_HEADS))
    return {
        'x_prompt': nrm(ks[0], (BATCH, SEQ, D_MODEL)),
        'x_sample': nrm(ks[1], (DEC_BATCH, DEC_SEQ, D_MODEL)),
        'norm_pre': 1.0 + 0.02 * nrm(ks[2], (DEPTH, D_MODEL)),
        'w_in': nrm(ks[3], (DEPTH, D_MODEL, IN_COLS)) * D_MODEL ** -0.5,
        'mlstm_gate_bias': FGATE_BIAS * gate_mask + 0.1 * nrm(ks[4], (DEPTH, 4 * MLSTM_HEADS)),
        'mlstm_conv_w': nrm(ks[5], (DEPTH, CONV_K, 2 * MLSTM_W)) * CONV_K ** -0.5,
        'mlstm_conv_b': 0.02 * nrm(ks[6], (DEPTH, 2 * MLSTM_W)),
        'mlstm_out_gain': 1.0 + 0.02 * nrm(ks[7], (DEPTH, MLSTM_W)),
        'gqa_q_norm': 1.0 + 0.02 * nrm(ks[8], (DEPTH, HEAD_DIM)),
        'gqa_k_norm': 1.0 + 0.02 * nrm(ks[9], (DEPTH, HEAD_DIM)),
        'w_out': nrm(ks[10], (DEPTH, MIX_W, D_MODEL)) * MIX_W ** -0.5,
        'norm_post': 1.0 + 0.02 * nrm(ks[11], (DEPTH, D_MODEL)),
    }


def reference(x_prompt, x_sample, norm_pre, w_in, mlstm_gate_bias, mlstm_conv_w, mlstm_conv_b,
              mlstm_out_gain, gqa_q_norm, gqa_k_norm, w_out, norm_post):
    y_prompt = trunk(x_prompt, norm_pre, w_in, mlstm_gate_bias, mlstm_conv_w, mlstm_conv_b,
                     mlstm_out_gain, gqa_q_norm, gqa_k_norm, w_out, norm_post)
    y_sample = trunk(x_sample, norm_pre, w_in, mlstm_gate_bias, mlstm_conv_w, mlstm_conv_b,
                     mlstm_out_gain, gqa_q_norm, gqa_k_norm, w_out, norm_post)
    return (y_prompt, y_sample)
```

```python
import functools
import math

import numpy as np
import jax
import jax.numpy as jnp
from jax import lax
from jax.experimental import pallas as pl
from jax.experimental.pallas import tpu as pltpu

F32 = jnp.float32
BF16 = jnp.bfloat16
HIGHEST = lax.Precision.HIGHEST

D_MODEL = 1024
HEAD_DIM = 64
NORM_EPS = 1e-6
GRID_W = 64
ROPE_THETA = 10000.0
MLSTM_HEADS = 4
MLSTM_W = MLSTM_HEADS * HEAD_DIM
MLSTM_CHUNK = 128
GQA_Q_HEADS = 6
GQA_KV_HEADS = 2
GQA_GROUP = GQA_Q_HEADS // GQA_KV_HEADS
GQA_W = GQA_Q_HEADS * HEAD_DIM
GQA_KV_W = GQA_KV_HEADS * HEAD_DIM
DIL_PATTERNS = ((128, 1), (512, 4), (2048, 16))
DIL_HEADS_PER_GROUP = 2
DIL_HEADS = DIL_HEADS_PER_GROUP * len(DIL_PATTERNS)
DIL_W = DIL_HEADS * HEAD_DIM
DIL_SIDE = 64
IN_SPLITS = (MLSTM_W, MLSTM_W, MLSTM_W, MLSTM_W, 4 * MLSTM_HEADS, MLSTM_W,
             GQA_W, GQA_KV_W, GQA_KV_W, GQA_W, DIL_W, DIL_W, DIL_W, DIL_W)

LANES = 128
SUBLANES = 8
GATE_PAD = LANES
M1_W = 3 * MLSTM_W + GATE_PAD
G1_W = GQA_W + 2 * GQA_KV_W
D1_W = 3 * DIL_W
Z_W = 2 * MLSTM_W + GQA_W + DIL_W
VMEM_LIMIT = 48 * 1024 * 1024
LOG2E = math.log2(math.e)
NEG_INF = float("-inf")


def _sigmoid(x):
    return 1.0 / (1.0 + jnp.exp(-x))


def _silu(x):
    return x * _sigmoid(x)


def _log_sigmoid(x):
    return jnp.minimum(x, 0.0) - jnp.log(1.0 + jnp.exp(-jnp.abs(x)))


def _head_id(shape, axis):
    return lax.broadcasted_iota(jnp.int32, shape, axis) // HEAD_DIM


def _params(sem):
    return pltpu.CompilerParams(dimension_semantics=sem, vmem_limit_bytes=VMEM_LIMIT)


def _inproj_kernel(x_ref, g_ref, wm_ref, wg_ref, wd_ref, wz_ref, om_ref, og_ref, od_ref, oz_ref):
    x = x_ref[...]
    ms = jnp.mean(x * x, axis=-1, keepdims=True)
    h = (x * lax.rsqrt(ms + NORM_EPS) * g_ref[...]).astype(BF16)
    om_ref[...] = jnp.dot(h, wm_ref[...], preferred_element_type=F32)
    og_ref[...] = jnp.dot(h, wg_ref[...], preferred_element_type=F32)
    od_ref[...] = jnp.dot(h, wd_ref[...], preferred_element_type=F32)
    oz_ref[...] = jnp.dot(h, wz_ref[...], preferred_element_type=F32)


def _inproj(x, g, wm, wg, wd, wz, tm=256):
    n = x.shape[0]
    row = lambda w: pl.BlockSpec((tm, w), lambda i: (i, 0))
    full = lambda a: pl.BlockSpec(a.shape, lambda i: (0, 0))
    return pl.pallas_call(
        _inproj_kernel,
        grid=(n // tm,),
        in_specs=[row(D_MODEL), full(g), full(wm), full(wg), full(wd), full(wz)],
        out_specs=[row(M1_W), row(G1_W), row(D1_W), row(Z_W)],
        out_shape=[jax.ShapeDtypeStruct((n, w), F32) for w in (M1_W, G1_W, D1_W, Z_W)],
        compiler_params=_params(("parallel",)),
        name="inproj",
    )(x, g, wm, wg, wd, wz)


def _expand_heads(cols, rows):
    hid = _head_id((rows, MLSTM_W), 1)
    out = jnp.broadcast_to(cols[MLSTM_HEADS - 1], (rows, MLSTM_W))
    for h in range(MLSTM_HEADS - 2, -1, -1):
        out = jnp.where(hid == h, cols[h], out)
    return out


def _mlstm_direction(x_ref, prev_ref, next_ref, out_ref, c_ref, n_ref, m_ref, cw_ref, cb_ref, gb_ref,
                     prev_valid, next_valid, d, forward):
    L = MLSTM_CHUNK
    W = MLSTM_W
    x = x_ref[0]
    qk = x[:, :2 * W]
    prev_row = prev_ref[0][SUBLANES - 1:SUBLANES, :] * prev_valid
    next_row = next_ref[0][0:1, :] * next_valid
    rid = lax.broadcasted_iota(jnp.int32, (L, 2 * W), 0)
    qk_m1 = jnp.where(rid == 0, prev_row, pltpu.roll(qk, 1, 0))
    qk_p1 = jnp.where(rid == L - 1, next_row, pltpu.roll(qk, L - 1, 0))
    cw = cw_ref[...]
    conv = qk_m1 * cw[0:1, :] + qk * cw[1:2, :] + qk_p1 * cw[2:3, :] + cb_ref[...]
    qk_act = _silu(conv)
    q = qk_act[:, :W]
    k = qk_act[:, W:] * HEAD_DIM ** -0.5
    v = x[:, 2 * W:3 * W]
    g = x[:, 3 * W:] + gb_ref[...]

    io = 0 if forward else 2 * MLSTM_HEADS
    fo = io + MLSTM_HEADS
    r_i = lax.broadcasted_iota(jnp.int32, (L, L), 0)
    c_i = lax.broadcasted_iota(jnp.int32, (L, L), 1)
    seen = (c_i <= r_i) if forward else (c_i >= r_i)
    cum = jnp.dot(seen.astype(F32), _log_sigmoid(g), precision=HIGHEST,
                  preferred_element_type=F32)
    gl = lax.broadcasted_iota(jnp.int32, (L, GATE_PAD), 1)
    rows_t = jnp.where((gl >= fo) & (gl < fo + MLSTM_HEADS), cum, g).T
    tot_row = L - 1 if forward else 0

    hid = _head_id((L, W), 1)
    qb = q.astype(BF16)
    kb = k.astype(BF16)
    m_state = m_ref[d][0:1, :]
    w_blocks, m_rows, w_inters, k_facs, decays, m_news = [], [], [], [], [], []
    for h in range(MLSTM_HEADS):
        bc = cum[:, fo + h:fo + h + 1]
        ic = g[:, io + h:io + h + 1]
        row_term = rows_t[io + h:io + h + 1, :] - rows_t[fo + h:fo + h + 1, :]
        dmat = jnp.where(seen, bc + row_term, NEG_INF)
        m_h = m_state[:, h * HEAD_DIM:h * HEAD_DIM + 1]
        d_inter = bc + m_h
        m_row = jnp.maximum(d_inter, jnp.max(dmat, axis=1, keepdims=True))
        s_h = lax.dot_general(jnp.where(hid == h, qb, jnp.zeros_like(qb)), kb,
                              (((1,), (1,)), ((), ())), preferred_element_type=F32)
        w_blocks.append((jnp.exp(dmat - m_row) * s_h).astype(BF16))
        b_tot = cum[tot_row:tot_row + 1, fo + h:fo + h + 1]
        gv = b_tot - bc + ic
        m_new = jnp.maximum(b_tot + m_h, jnp.max(gv, axis=0, keepdims=True))
        m_rows.append(m_row)
        w_inters.append(jnp.exp(d_inter - m_row))
        k_facs.append(jnp.exp(gv - m_new))
        decays.append(jnp.exp(b_tot + m_h - m_new))
        m_news.append(m_new)

    m_row_e = _expand_heads(m_rows, L)
    w_inter_e = _expand_heads(w_inters, L)
    k_fac_e = _expand_heads(k_facs, L)
    decay_e = _expand_heads(decays, 1)
    m_new_e = _expand_heads(m_news, 1)

    w_cat = jnp.concatenate(w_blocks, axis=1)
    v_stack = jnp.concatenate([jnp.where(hid == h, v, 0.0) for h in range(MLSTM_HEADS)],
                              axis=0).astype(BF16)
    sr = lax.broadcasted_iota(jnp.int32, (MLSTM_HEADS * L, W), 0) // L
    ones_stack = (sr == _head_id((MLSTM_HEADS * L, W), 1)).astype(BF16)
    same_head = _head_id((W, W), 0) == _head_id((W, W), 1)

    c_state = c_ref[d]
    n_state = n_ref[d][0:1, :]
    q_c = jnp.dot(qb, c_state.astype(BF16), preferred_element_type=F32)
    q_n = jnp.dot((q * n_state).astype(BF16), same_head.astype(BF16), preferred_element_type=F32)
    num = w_inter_e * q_c + jnp.dot(w_cat, v_stack, preferred_element_type=F32)
    den = w_inter_e * q_n + jnp.dot(w_cat, ones_stack, preferred_element_type=F32)
    out_ref[0] = num / jnp.maximum(jnp.abs(den), jnp.exp(-m_row_e))

    kw = k_fac_e * k
    kv = lax.dot_general(kw.astype(BF16), v.astype(BF16), (((0,), (0,)), ((), ())),
                         preferred_element_type=F32)
    c_ref[d] = decay_e * c_state + jnp.where(same_head, kv, 0.0)
    n_new = decay_e * n_state + jnp.sum(kw, axis=0, keepdims=True)
    n_ref[d] = jnp.broadcast_to(n_new, (SUBLANES, W))
    m_ref[d] = jnp.broadcast_to(m_new_e, (SUBLANES, W))


def _mlstm_kernel(xf_ref, pf_ref, nf_ref, xb_ref, pb_ref, nb_ref, cw_ref, cb_ref, gb_ref,
                  hf_ref, hb_ref, c_ref, n_ref, m_ref, *, nc):
    c = pl.program_id(1)

    @pl.when(c == 0)
    def _():
        c_ref[...] = jnp.zeros_like(c_ref)
        n_ref[...] = jnp.zeros_like(n_ref)
        m_ref[...] = jnp.zeros_like(m_ref)

    not_first = jnp.where(c > 0, 1.0, 0.0).astype(F32)
    not_last = jnp.where(c < nc - 1, 1.0, 0.0).astype(F32)
    _mlstm_direction(xf_ref, pf_ref, nf_ref, hf_ref, c_ref, n_ref, m_ref, cw_ref, cb_ref, gb_ref,
                     not_first, not_last, 0, True)
    _mlstm_direction(xb_ref, pb_ref, nb_ref, hb_ref, c_ref, n_ref, m_ref, cw_ref, cb_ref, gb_ref,
                     not_last, not_first, 1, False)


def _mlstm(m1, conv_w, conv_b, gate_bias):
    B, T, _ = m1.shape
    L = MLSTM_CHUNK
    nc = T // L
    hb = L // SUBLANES
    last_halo = T // SUBLANES - 1
    qkw = 2 * MLSTM_W
    fwd = lambda c: c
    bwd = lambda c: nc - 1 - c

    def specs(pos):
        return [
            pl.BlockSpec((1, L, M1_W), lambda b, c: (b, pos(c), 0)),
            pl.BlockSpec((1, SUBLANES, qkw), lambda b, c: (b, jnp.maximum(pos(c) * hb - 1, 0), 0)),
            pl.BlockSpec((1, SUBLANES, qkw), lambda b, c: (b, jnp.minimum((pos(c) + 1) * hb, last_halo), 0)),
        ]

    full = lambda a: pl.BlockSpec(a.shape, lambda b, c: (0, 0))
    return pl.pallas_call(
        functools.partial(_mlstm_kernel, nc=nc),
        grid=(B, nc),
        in_specs=specs(fwd) + specs(bwd) + [full(conv_w), full(conv_b), full(gate_bias)],
        out_specs=[pl.BlockSpec((1, L, MLSTM_W), lambda b, c: (b, c, 0)),
                   pl.BlockSpec((1, L, MLSTM_W), lambda b, c: (b, nc - 1 - c, 0))],
        out_shape=[jax.ShapeDtypeStruct((B, T, MLSTM_W), F32)] * 2,
        scratch_shapes=[pltpu.VMEM((2, MLSTM_W, MLSTM_W), F32),
                        pltpu.VMEM((2, SUBLANES, MLSTM_W), F32),
                        pltpu.VMEM((2, SUBLANES, MLSTM_W), F32)],
        compiler_params=_params(("parallel", "arbitrary")),
        name="mlstm",
    )(m1, m1, m1, m1, m1, m1, conv_w, conv_b, gate_bias)


def _rope_tile(a, cos, sin):
    half = HEAD_DIM // 2
    lane = lax.broadcasted_iota(jnp.int32, a.shape, 1) % HEAD_DIM
    rot = jnp.where(lane < half, pltpu.roll(a, LANES - half, 1), pltpu.roll(a, half, 1))
    return a * cos + rot * sin


def _head_rms(a, gain):
    w = a.shape[1]
    same_head = (_head_id((w, w), 0) == _head_id((w, w), 1)).astype(F32)
    ss = jnp.dot(a * a, same_head, precision=HIGHEST, preferred_element_type=F32)
    return a * lax.rsqrt(ss * (1.0 / HEAD_DIM) + NORM_EPS) * gain


def _gqa_prep_kernel(g_ref, cos_ref, sin_ref, qn_ref, kn_ref, q_out, kt_out, v_out):
    x = g_ref[0]
    cos = cos_ref[...]
    sin = sin_ref[...]
    q = _head_rms(x[:, :GQA_W], qn_ref[...])
    k = _head_rms(x[:, GQA_W:GQA_W + GQA_KV_W], kn_ref[...])
    v = x[:, GQA_W + GQA_KV_W:]
    q_scale = HEAD_DIM ** -0.5 * LOG2E
    q_rot = jnp.concatenate(
        [_rope_tile(q[:, j * LANES:(j + 1) * LANES], cos, sin) for j in range(GQA_W // LANES)], axis=1)
    q_out[0] = (q_rot * q_scale).astype(BF16)
    k_rot = _rope_tile(k, cos, sin)
    kt = k_rot.T.astype(BF16)
    kt_out[0] = kt.reshape(GQA_KV_HEADS, HEAD_DIM, kt.shape[1])
    lane = lax.broadcasted_iota(jnp.int32, v.shape, 1)
    v_out[0, 0] = jnp.where(lane < HEAD_DIM, v, 1.0).astype(BF16)
    v_out[0, 1] = jnp.where(lane < HEAD_DIM, pltpu.roll(v, HEAD_DIM, 1), 1.0).astype(BF16)


def _gqa_prep(g1, cos, sin, q_norm, k_norm, tp=512):
    B, T, _ = g1.shape
    full = lambda a: pl.BlockSpec(a.shape, lambda b, i: (0, 0))
    return pl.pallas_call(
        _gqa_prep_kernel,
        grid=(B, T // tp),
        in_specs=[pl.BlockSpec((1, tp, G1_W), lambda b, i: (b, i, 0)),
                  pl.BlockSpec((tp, LANES), lambda b, i: (i, 0)),
                  pl.BlockSpec((tp, LANES), lambda b, i: (i, 0)),
                  full(q_norm), full(k_norm)],
        out_specs=[pl.BlockSpec((1, tp, GQA_W), lambda b, i: (b, i, 0)),
                   pl.BlockSpec((1, GQA_KV_HEADS, HEAD_DIM, tp), lambda b, i: (b, 0, 0, i)),
                   pl.BlockSpec((1, GQA_KV_HEADS, tp, LANES), lambda b, i: (b, 0, i, 0))],
        out_shape=[jax.ShapeDtypeStruct((B, T, GQA_W), BF16),
                   jax.ShapeDtypeStruct((B, GQA_KV_HEADS, HEAD_DIM, T), BF16),
                   jax.ShapeDtypeStruct((B, GQA_KV_HEADS, T, LANES), BF16)],
        compiler_params=_params(("parallel", "parallel")),
        name="gqa_prep",
    )(g1, cos, sin, q_norm, k_norm)


def _gqa_flash_kernel(q_ref, kt_ref, v_ref, o_ref, qs_ref, m_ref, acc_ref, *, tq, nk):
    kh = pl.program_id(2)
    ki = pl.program_id(3)

    @pl.when(ki == 0)
    def _():
        for kk in range(GQA_KV_HEADS):
            @pl.when(kh == kk)
            def _():
                for h in range(GQA_GROUP):
                    lo = (kk * GQA_GROUP + h) * HEAD_DIM
                    qs_ref[h * tq:(h + 1) * tq, :] = q_ref[0, :, lo:lo + HEAD_DIM]
        m_ref[...] = jnp.full_like(m_ref, NEG_INF)
        acc_ref[...] = jnp.zeros_like(acc_ref)

    s = jnp.dot(qs_ref[...], kt_ref[0, 0], preferred_element_type=F32)
    m_prev = m_ref[...]
    m_new = jnp.maximum(m_prev, jnp.max(s, axis=-1, keepdims=True))
    alpha = jnp.exp2(m_prev - m_new)
    p = jnp.exp2(s - m_new[:, :1]).astype(BF16)
    acc_ref[...] = alpha * acc_ref[...] + jnp.dot(p, v_ref[0, 0], preferred_element_type=F32)
    m_ref[...] = m_new

    @pl.when(ki == nk - 1)
    def _():
        acc = acc_ref[...]
        o = acc / pltpu.roll(acc, HEAD_DIM, 1)
        for kk in range(GQA_KV_HEADS):
            @pl.when(kh == kk)
            def _():
                for h in range(GQA_GROUP):
                    lo = (kk * GQA_GROUP + h) * HEAD_DIM
                    o_ref[0, :, lo:lo + HEAD_DIM] = o[h * tq:(h + 1) * tq, :HEAD_DIM].astype(o_ref.dtype)


def _gqa_flash(q, kt, v, tq=256, tk=512):
    B, T, _ = q.shape
    nq, nk = T // tq, T // tk
    m_rows = GQA_GROUP * tq
    return pl.pallas_call(
        functools.partial(_gqa_flash_kernel, tq=tq, nk=nk),
        grid=(B, nq, GQA_KV_HEADS, nk),
        in_specs=[pl.BlockSpec((1, tq, GQA_W), lambda b, i, h, j: (b, i, 0)),
                  pl.BlockSpec((1, 1, HEAD_DIM, tk), lambda b, i, h, j: (b, h, 0, j)),
                  pl.BlockSpec((1, 1, tk, LANES), lambda b, i, h, j: (b, h, j, 0))],
        out_specs=pl.BlockSpec((1, tq, GQA_W), lambda b, i, h, j: (b, i, 0)),
        out_shape=jax.ShapeDtypeStruct((B, T, GQA_W), BF16),
        scratch_shapes=[pltpu.VMEM((m_rows, HEAD_DIM), BF16),
                        pltpu.VMEM((m_rows, LANES), F32),
                        pltpu.VMEM((m_rows, LANES), F32)],
        compiler_params=_params(("parallel", "parallel", "arbitrary", "arbitrary")),
        name="gqa_flash",
    )(q, kt, v)


def _dil_kernel(q_ref, k_ref, v_ref, o_ref, lse_ref, *, S, bq, kw, dilation, slopes):
    p0 = pl.program_id(2) * bq
    start = pl.multiple_of(jnp.clip(p0 - DIL_SIDE, 0, S - kw), DIL_SIDE)
    q = q_ref[0, 0]
    kwin = k_ref[0, 0, pl.ds(start, kw), :]
    vwin = v_ref[0, 0, pl.ds(start, kw), :]
    qpos = p0 + lax.broadcasted_iota(jnp.int32, (bq, kw), 0)
    kpos = start + lax.broadcasted_iota(jnp.int32, (bq, kw), 1)
    rel = jnp.abs(kpos - qpos)
    valid = rel <= DIL_SIDE
    dist = (rel * dilation).astype(F32)
    hid = _head_id((bq, LANES), 1)
    outs, lses = [], []
    for hh in range(DIL_HEADS_PER_GROUP):
        qm = jnp.where(hid == hh, q, jnp.zeros_like(q))
        s = lax.dot_general(qm, kwin, (((1,), (1,)), ((), ())), preferred_element_type=F32)
        s = jnp.where(valid, s * HEAD_DIM ** -0.5 - slopes[hh] * dist, NEG_INF)
        m = jnp.max(s, axis=-1, keepdims=True)
        p = jnp.exp(s - m)
        l = jnp.sum(p, axis=-1, keepdims=True)
        outs.append(jnp.dot(p.astype(BF16), vwin, preferred_element_type=F32) / l)
        lses.append(m + jnp.log(l))
    o_ref[0, 0] = jnp.where(hid == 0, outs[0], outs[1])
    lse_ref[0, 0] = jnp.where(hid == 0, lses[0], lses[1])


def _dilated(q, k, v, dilation, slopes):
    B, d, S, _ = q.shape
    bq = min(128, S)
    kw = min(bq + 2 * DIL_SIDE, S)
    seq = pl.BlockSpec((1, 1, S, LANES), lambda b, r, i: (b, r, 0, 0))
    blk = pl.BlockSpec((1, 1, bq, LANES), lambda b, r, i: (b, r, i, 0))
    return pl.pallas_call(
        functools.partial(_dil_kernel, S=S, bq=bq, kw=kw, dilation=dilation, slopes=slopes),
        grid=(B, d, S // bq),
        in_specs=[blk, seq, seq],
        out_specs=[blk, blk],
        out_shape=[jax.ShapeDtypeStruct((B, d, S, LANES), F32)] * 2,
        compiler_params=_params(("parallel", "parallel", "arbitrary")),
        name=f"dilated_{dilation}",
    )(q, k, v)


def _outproj_kernel(x_ref, hf_ref, hb_ref, z_ref, b_ref, o0_ref, o1_ref, o2_ref, l0_ref, l1_ref, l2_ref,
                    gain_ref, w_ref, gpost_ref, out_ref):
    z = z_ref[...]
    mo = z[:, :MLSTM_W]
    mz = z[:, MLSTM_W:2 * MLSTM_W]
    gz = z[:, 2 * MLSTM_W:2 * MLSTM_W + GQA_W]
    dz = z[:, 2 * MLSTM_W + GQA_W:]

    h = _sigmoid(mo) * (hf_ref[...] + hb_ref[...])
    same_head = (_head_id((MLSTM_W, MLSTM_W), 0) == _head_id((MLSTM_W, MLSTM_W), 1)).astype(F32)
    ss = jnp.dot(h * h, same_head, precision=HIGHEST, preferred_element_type=F32)
    a = h * lax.rsqrt(ss * (1.0 / HEAD_DIM) + NORM_EPS) * gain_ref[...] * _silu(mz)

    b = b_ref[...].astype(F32) * _silu(gz)

    lses = [l0_ref[...], l1_ref[...], l2_ref[...]]
    outs = [o0_ref[...], o1_ref[...], o2_ref[...]]
    mx = jnp.maximum(jnp.maximum(lses[0], lses[1]), lses[2])
    es = [jnp.exp(l - mx) for l in lses]
    inv = 1.0 / (es[0] + es[1] + es[2])
    c = jnp.concatenate([outs[g] * (es[g] * inv) for g in range(3)], axis=1) * _silu(dz)

    w = w_ref[...]
    y = jnp.dot(a.astype(BF16), w[:MLSTM_W], preferred_element_type=F32)
    y += jnp.dot(b.astype(BF16), w[MLSTM_W:MLSTM_W + GQA_W], preferred_element_type=F32)
    y += jnp.dot(c.astype(BF16), w[MLSTM_W + GQA_W:], preferred_element_type=F32)
    ms = jnp.mean(y * y, axis=-1, keepdims=True)
    out_ref[...] = x_ref[...] + y * lax.rsqrt(ms + NORM_EPS) * gpost_ref[...]


def _outproj(x, hf, hb, z, b, dil_o, dil_l, gain, w, gpost, tm=256):
    n = x.shape[0]
    row = lambda a: pl.BlockSpec((tm, a.shape[1]), lambda i: (i, 0))
    full = lambda a: pl.BlockSpec(a.shape, lambda i: (0, 0))
    rows = [x, hf, hb, z, b] + list(dil_o) + list(dil_l)
    consts = [gain, w, gpost]
    return pl.pallas_call(
        _outproj_kernel,
        grid=(n // tm,),
        in_specs=[row(a) for a in rows] + [full(a) for a in consts],
        out_specs=pl.BlockSpec((tm, D_MODEL), lambda i: (i, 0)),
        out_shape=jax.ShapeDtypeStruct((n, D_MODEL), F32),
        compiler_params=_params(("parallel",)),
        name="outproj",
    )(*rows, *consts)


def _rope_tables(T):
    rows = T // GRID_W
    r, c = jnp.meshgrid(jnp.arange(rows), jnp.arange(GRID_W), indexing="ij")
    r = r.reshape(-1).astype(F32)
    c = c.reshape(-1).astype(F32)
    axis_dim = HEAD_DIM // 2
    inv = ROPE_THETA ** (-jnp.arange(0, axis_dim, 2, dtype=F32) / axis_dim)
    ang = jnp.concatenate([r[:, None] * inv, c[:, None] * inv], axis=-1)
    cos, sin = jnp.cos(ang), jnp.sin(ang)
    cos_t = jnp.tile(jnp.concatenate([cos, cos], axis=-1), (1, LANES // HEAD_DIM))
    sin_t = jnp.tile(jnp.concatenate([-sin, sin], axis=-1), (1, LANES // HEAD_DIM))
    return cos_t, sin_t


def _split_w_in(w):
    cuts = np.cumsum((0,) + IN_SPLITS)
    mq, mk, mv, mo, mg, mz, gq, gk, gv, gz, dq, dk, dv, dz = [w[:, cuts[i]:cuts[i + 1]] for i in range(14)]
    pad = jnp.zeros((w.shape[0], GATE_PAD - mg.shape[1]), w.dtype)
    cat = lambda *xs: jnp.concatenate(xs, axis=1).astype(BF16)
    return cat(mq, mk, mv, mg, pad), cat(gq, gk, gv), cat(dq, dk, dv), cat(mo, mz, gz, dz)


def _alibi_slopes():
    return [float(np.exp2(np.float32(-8.0) * np.float32(i) / np.float32(DIL_HEADS)))
            for i in range(1, DIL_HEADS + 1)]


def _layer(x, B, T, p, cos, sin, slopes):
    n = B * T
    m1, g1, d1, z = _inproj(x, p["norm_pre"], *p["w_in"])
    hf, hb = _mlstm(m1.reshape(B, T, M1_W), p["conv_w"], p["conv_b"], p["gate_bias"])
    q, kt, v = _gqa_prep(g1.reshape(B, T, G1_W), cos, sin, p["q_norm"], p["k_norm"])
    b = _gqa_flash(q, kt, v)

    d1 = d1.reshape(B, T, D1_W)
    dil_o, dil_l = [], []
    for g, (_, dilation) in enumerate(DIL_PATTERNS):
        S = T // dilation

        def classes(off):
            a = d1[:, :, off + g * LANES:off + (g + 1) * LANES].astype(BF16)
            return a.reshape(B, S, dilation, LANES).transpose(0, 2, 1, 3)

        o_g, l_g = _dilated(classes(0), classes(DIL_W), classes(2 * DIL_W), dilation,
                            slopes[g * DIL_HEADS_PER_GROUP:(g + 1) * DIL_HEADS_PER_GROUP])
        back = lambda a: a.transpose(0, 2, 1, 3).reshape(n, LANES)
        dil_o.append(back(o_g))
        dil_l.append(back(l_g))

    return _outproj(x, hf.reshape(n, MLSTM_W), hb.reshape(n, MLSTM_W), z, b.reshape(n, GQA_W),
                    dil_o, dil_l, p["out_gain"], p["w_out"], p["norm_post"])


def _trunk(x, layers):
    B, T, _ = x.shape
    cos, sin = _rope_tables(T)
    slopes = _alibi_slopes()
    x = x.reshape(B * T, D_MODEL)
    for p in layers:
        x = _layer(x, B, T, p, cos, sin, slopes)
    return x.reshape(B, T, D_MODEL)


def kernel(x_prompt, x_sample, norm_pre, w_in, mlstm_gate_bias, mlstm_conv_w, mlstm_conv_b, mlstm_out_gain,
           gqa_q_norm, gqa_k_norm, w_out, norm_post):
    depth = w_in.shape[0]
    row = lambda a: a.reshape(1, -1).astype(F32)
    layers = []
    for l in range(depth):
        gate_bias = jnp.concatenate(
            [mlstm_gate_bias[l].astype(F32), jnp.zeros((GATE_PAD - 4 * MLSTM_HEADS,), F32)]).reshape(1, GATE_PAD)
        layers.append(dict(
            norm_pre=row(norm_pre[l]),
            w_in=_split_w_in(w_in[l]),
            gate_bias=gate_bias,
            conv_w=mlstm_conv_w[l].astype(F32),
            conv_b=row(mlstm_conv_b[l]),
            out_gain=row(mlstm_out_gain[l]),
            q_norm=row(jnp.tile(gqa_q_norm[l], GQA_Q_HEADS)),
            k_norm=row(jnp.tile(gqa_k_norm[l], GQA_KV_HEADS)),
            w_out=w_out[l].astype(BF16),
            norm_post=row(norm_post[l]),
        ))
    return (_trunk(x_prompt, layers), _trunk(x_sample, layers))
```

```python
import functools
import math

import numpy as np
import jax
import jax.numpy as jnp
from jax import lax
from jax.experimental import pallas as pl
from jax.experimental.pallas import tpu as pltpu

F32 = jnp.float32
BF16 = jnp.bfloat16
HIGHEST = lax.Precision.HIGHEST

D_MODEL = 1024
HEAD_DIM = 64
NORM_EPS = 1e-6
GRID_W = 64
ROPE_THETA = 10000.0
MLSTM_HEADS = 4
MLSTM_W = MLSTM_HEADS * HEAD_DIM
MLSTM_CHUNK = 128
GQA_Q_HEADS = 6
GQA_KV_HEADS = 2
GQA_GROUP = GQA_Q_HEADS // GQA_KV_HEADS
GQA_W = GQA_Q_HEADS * HEAD_DIM
GQA_KV_W = GQA_KV_HEADS * HEAD_DIM
DIL_PATTERNS = ((128, 1), (512, 4), (2048, 16))
DIL_HEADS_PER_GROUP = 2
DIL_HEADS = DIL_HEADS_PER_GROUP * len(DIL_PATTERNS)
DIL_W = DIL_HEADS * HEAD_DIM
DIL_SIDE = 64
IN_SPLITS = (MLSTM_W, MLSTM_W, MLSTM_W, MLSTM_W, 4 * MLSTM_HEADS, MLSTM_W,
             GQA_W, GQA_KV_W, GQA_KV_W, GQA_W, DIL_W, DIL_W, DIL_W, DIL_W)

LANES = 128
SUBLANES = 8
GATE_PAD = LANES
M1_W = 3 * MLSTM_W + GATE_PAD
G1_W = GQA_W + 2 * GQA_KV_W
D1_W = 3 * DIL_W
Z_W = 2 * MLSTM_W + GQA_W + DIL_W
VMEM_LIMIT = 48 * 1024 * 1024
LOG2E = math.log2(math.e)
NEG_INF = float("-inf")


def _sigmoid(x):
    return 1.0 / (1.0 + jnp.exp(-x))


def _silu(x):
    return x * _sigmoid(x)


def _log_sigmoid(x):
    return jnp.minimum(x, 0.0) - jnp.log(1.0 + jnp.exp(-jnp.abs(x)))


def _head_id(shape, axis):
    return lax.broadcasted_iota(jnp.int32, shape, axis) // HEAD_DIM


def _params(sem):
    return pltpu.CompilerParams(dimension_semantics=sem, vmem_limit_bytes=VMEM_LIMIT)


def _inproj_kernel(x_ref, g_ref, wm_ref, wg_ref, wd_ref, wz_ref, om_ref, og_ref, od_ref, oz_ref):
    x = x_ref[...]
    ms = jnp.mean(x * x, axis=-1, keepdims=True)
    h = (x * lax.rsqrt(ms + NORM_EPS) * g_ref[...]).astype(BF16)
    om_ref[...] = jnp.dot(h, wm_ref[...], preferred_element_type=F32)
    og_ref[...] = jnp.dot(h, wg_ref[...], preferred_element_type=F32)
    od_ref[...] = jnp.dot(h, wd_ref[...], preferred_element_type=F32)
    oz_ref[...] = jnp.dot(h, wz_ref[...], preferred_element_type=F32)


def _inproj(x, g, wm, wg, wd, wz, tm=256):
    n = x.shape[0]
    row = lambda w: pl.BlockSpec((tm, w), lambda i: (i, 0))
    full = lambda a: pl.BlockSpec(a.shape, lambda i: (0, 0))
    return pl.pallas_call(
        _inproj_kernel,
        grid=(n // tm,),
        in_specs=[row(D_MODEL), full(g), full(wm), full(wg), full(wd), full(wz)],
        out_specs=[row(M1_W), row(G1_W), row(D1_W), row(Z_W)],
        out_shape=[jax.ShapeDtypeStruct((n, w), F32) for w in (M1_W, G1_W, D1_W, Z_W)],
        compiler_params=_params(("parallel",)),
        name="inproj",
    )(x, g, wm, wg, wd, wz)


def _expand_heads(cols, rows):
    hid = _head_id((rows, MLSTM_W), 1)
    out = jnp.broadcast_to(cols[MLSTM_HEADS - 1], (rows, MLSTM_W))
    for h in range(MLSTM_HEADS - 2, -1, -1):
        out = jnp.where(hid == h, cols[h], out)
    return out


def _mlstm_direction(x_ref, prev_ref, next_ref, out_ref, c_ref, n_ref, m_ref, cw_ref, cb_ref, gb_ref,
                     prev_valid, next_valid, d, forward):
    L = MLSTM_CHUNK
    W = MLSTM_W
    x = x_ref[0]
    qk = x[:, :2 * W]
    prev_row = prev_ref[0][SUBLANES - 1:SUBLANES, :] * prev_valid
    next_row = next_ref[0][0:1, :] * next_valid
    rid = lax.broadcasted_iota(jnp.int32, (L, 2 * W), 0)
    qk_m1 = jnp.where(rid == 0, prev_row, pltpu.roll(qk, 1, 0))
    qk_p1 = jnp.where(rid == L - 1, next_row, pltpu.roll(qk, L - 1, 0))
    cw = cw_ref[...]
    conv = qk_m1 * cw[0:1, :] + qk * cw[1:2, :] + qk_p1 * cw[2:3, :] + cb_ref[...]
    qk_act = _silu(conv)
    q = qk_act[:, :W]
    k = qk_act[:, W:] * HEAD_DIM ** -0.5
    v = x[:, 2 * W:3 * W]
    g = x[:, 3 * W:] + gb_ref[...]

    io = 0 if forward else 2 * MLSTM_HEADS
    fo = io + MLSTM_HEADS
    r_i = lax.broadcasted_iota(jnp.int32, (L, L), 0)
    c_i = lax.broadcasted_iota(jnp.int32, (L, L), 1)
    seen = (c_i <= r_i) if forward else (c_i >= r_i)
    cum = jnp.dot(seen.astype(F32), _log_sigmoid(g), precision=HIGHEST,
                  preferred_element_type=F32)
    gl = lax.broadcasted_iota(jnp.int32, (L, GATE_PAD), 1)
    rows_t = jnp.where((gl >= fo) & (gl < fo + MLSTM_HEADS), cum, g).T
    tot_row = L - 1 if forward else 0

    hid = _head_id((L, W), 1)
    qb = q.astype(BF16)
    kb = k.astype(BF16)
    m_state = m_ref[d][0:1, :]
    w_blocks, m_rows, w_inters, k_facs, decays, m_news = [], [], [], [], [], []
    for h in range(MLSTM_HEADS):
        bc = cum[:, fo + h:fo + h + 1]
        ic = g[:, io + h:io + h + 1]
        row_term = rows_t[io + h:io + h + 1, :] - rows_t[fo + h:fo + h + 1, :]
        dmat = jnp.where(seen, bc + row_term, NEG_INF)
        m_h = m_state[:, h * HEAD_DIM:h * HEAD_DIM + 1]
        d_inter = bc + m_h
        m_row = jnp.maximum(d_inter, jnp.max(dmat, axis=1, keepdims=True))
        s_h = lax.dot_general(jnp.where(hid == h, qb, jnp.zeros_like(qb)), kb,
                              (((1,), (1,)), ((), ())), preferred_element_type=F32)
        w_blocks.append((jnp.exp(dmat - m_row) * s_h).astype(BF16))
        b_tot = cum[tot_row:tot_row + 1, fo + h:fo + h + 1]
        gv = b_tot - bc + ic
        m_new = jnp.maximum(b_tot + m_h, jnp.max(gv, axis=0, keepdims=True))
        m_rows.append(m_row)
        w_inters.append(jnp.exp(d_inter - m_row))
        k_facs.append(jnp.exp(gv - m_new))
        decays.append(jnp.exp(b_tot + m_h - m_new))
        m_news.append(m_new)

    m_row_e = _expand_heads(m_rows, L)
    w_inter_e = _expand_heads(w_inters, L)
    k_fac_e = _expand_heads(k_facs, L)
    decay_e = _expand_heads(decays, 1)
    m_new_e = _expand_heads(m_news, 1)

    w_cat = jnp.concatenate(w_blocks, axis=1)
    v_stack = jnp.concatenate([jnp.where(hid == h, v, 0.0) for h in range(MLSTM_HEADS)],
                              axis=0).astype(BF16)
    sr = lax.broadcasted_iota(jnp.int32, (MLSTM_HEADS * L, W), 0) // L
    ones_stack = (sr == _head_id((MLSTM_HEADS * L, W), 1)).astype(BF16)
    same_head = _head_id((W, W), 0) == _head_id((W, W), 1)

    c_state = c_ref[d]
    n_state = n_ref[d][0:1, :]
    q_c = jnp.dot(qb, c_state.astype(BF16), preferred_element_type=F32)
    q_n = jnp.dot((q * n_state).astype(BF16), same_head.astype(BF16), preferred_element_type=F32)
    num = w_inter_e * q_c + jnp.dot(w_cat, v_stack, preferred_element_type=F32)
    den = w_inter_e * q_n + jnp.dot(w_cat, ones_stack, preferred_element_type=F32)
    out_ref[0] = num / jnp.maximum(jnp.abs(den), jnp.exp(-m_row_e))

    kw = k_fac_e * k
    kv = lax.dot_general(kw.astype(BF16), v.astype(BF16), (((0,), (0,)), ((), ())),
                         preferred_element_type=F32)
    c_ref[d] = decay_e * c_state + jnp.where(same_head, kv, 0.0)
    n_new = decay_e * n_state + jnp.sum(kw, axis=0, keepdims=True)
    n_ref[d] = jnp.broadcast_to(n_new, (SUBLANES, W))
    m_ref[d] = jnp.broadcast_to(m_new_e, (SUBLANES, W))


def _mlstm_kernel(xf_ref, pf_ref, nf_ref, xb_ref, pb_ref, nb_ref, cw_ref, cb_ref, gb_ref,
                  hf_ref, hb_ref, c_ref, n_ref, m_ref, *, nc):
    c = pl.program_id(1)

    @pl.when(c == 0)
    def _():
        c_ref[...] = jnp.zeros_like(c_ref)
        n_ref[...] = jnp.zeros_like(n_ref)
        m_ref[...] = jnp.zeros_like(m_ref)

    not_first = jnp.where(c > 0, 1.0, 0.0).astype(F32)
    not_last = jnp.where(c < nc - 1, 1.0, 0.0).astype(F32)
    _mlstm_direction(xf_ref, pf_ref, nf_ref, hf_ref, c_ref, n_ref, m_ref, cw_ref, cb_ref, gb_ref,
                     not_first, not_last, 0, True)
    _mlstm_direction(xb_ref, pb_ref, nb_ref, hb_ref, c_ref, n_ref, m_ref, cw_ref, cb_ref, gb_ref,
                     not_last, not_first, 1, False)


def _mlstm(m1, conv_w, conv_b, gate_bias):
    B, T, _ = m1.shape
    L = MLSTM_CHUNK
    nc = T // L
    hb = L // SUBLANES
    last_halo = T // SUBLANES - 1
    qkw = 2 * MLSTM_W
    fwd = lambda c: c
    bwd = lambda c: nc - 1 - c

    def specs(pos):
        return [
            pl.BlockSpec((1, L, M1_W), lambda b, c: (b, pos(c), 0)),
            pl.BlockSpec((1, SUBLANES, qkw), lambda b, c: (b, jnp.maximum(pos(c) * hb - 1, 0), 0)),
            pl.BlockSpec((1, SUBLANES, qkw), lambda b, c: (b, jnp.minimum((pos(c) + 1) * hb, last_halo), 0)),
        ]

    full = lambda a: pl.BlockSpec(a.shape, lambda b, c: (0, 0))
    return pl.pallas_call(
        functools.partial(_mlstm_kernel, nc=nc),
        grid=(B, nc),
        in_specs=specs(fwd) + specs(bwd) + [full(conv_w), full(conv_b), full(gate_bias)],
        out_specs=[pl.BlockSpec((1, L, MLSTM_W), lambda b, c: (b, c, 0)),
                   pl.BlockSpec((1, L, MLSTM_W), lambda b, c: (b, nc - 1 - c, 0))],
        out_shape=[jax.ShapeDtypeStruct((B, T, MLSTM_W), F32)] * 2,
        scratch_shapes=[pltpu.VMEM((2, MLSTM_W, MLSTM_W), F32),
                        pltpu.VMEM((2, SUBLANES, MLSTM_W), F32),
                        pltpu.VMEM((2, SUBLANES, MLSTM_W), F32)],
        compiler_params=_params(("parallel", "arbitrary")),
        name="mlstm",
    )(m1, m1, m1, m1, m1, m1, conv_w, conv_b, gate_bias)


GQA_SAFE_LOG2 = 56.0
GQA_BOUND_SLACK = 1.0 + 2.0 ** -6
GQA_AUG_W = GQA_Q_HEADS * LANES


def _rope_tile(a, cos, sin):
    half = HEAD_DIM // 2
    lane = lax.broadcasted_iota(jnp.int32, a.shape, 1) % HEAD_DIM
    rot = jnp.where(lane < half, pltpu.roll(a, LANES - half, 1), pltpu.roll(a, half, 1))
    return a * cos + rot * sin


def _head_sumsq(a):
    w = a.shape[1]
    same_head = (_head_id((w, w), 0) == _head_id((w, w), 1)).astype(F32)
    return jnp.dot(a * a, same_head, precision=HIGHEST, preferred_element_type=F32)


def _head_rms(a, gain):
    return a * lax.rsqrt(_head_sumsq(a) * (1.0 / HEAD_DIM) + NORM_EPS) * gain


def _gqa_prep_k_kernel(g_ref, cos_ref, sin_ref, kn_ref, kt_out, v_out, kss_out):
    x = g_ref[0]
    k = _head_rms(x[:, GQA_W:GQA_W + GQA_KV_W], kn_ref[...])
    v = x[:, GQA_W + GQA_KV_W:]
    k_rot = _rope_tile(k, cos_ref[...], sin_ref[...])
    lane = lax.broadcasted_iota(jnp.int32, k_rot.shape, 1)
    ones_col = jnp.where(lane == HEAD_DIM, 1.0, 0.0)
    first = lane < HEAD_DIM
    kt_out[0, 0] = jnp.where(first, k_rot, ones_col).T.astype(BF16)
    kt_out[0, 1] = jnp.where(first, pltpu.roll(k_rot, HEAD_DIM, 1), ones_col).T.astype(BF16)
    v_out[0, 0] = jnp.where(first, v, 1.0).astype(BF16)
    v_out[0, 1] = jnp.where(first, pltpu.roll(v, HEAD_DIM, 1), 1.0).astype(BF16)
    kss = jnp.max(_head_sumsq(k_rot), axis=0, keepdims=True)
    kss_out[0, 0] = jnp.broadcast_to(kss, (SUBLANES, LANES))


def _gqa_prep_k(g1, cos, sin, k_norm, tp=512):
    B, T, _ = g1.shape
    nt = T // tp
    full = lambda a: pl.BlockSpec(a.shape, lambda b, i: (0, 0))
    return pl.pallas_call(
        _gqa_prep_k_kernel,
        grid=(B, nt),
        in_specs=[pl.BlockSpec((1, tp, G1_W), lambda b, i: (b, i, 0)),
                  pl.BlockSpec((tp, LANES), lambda b, i: (i, 0)),
                  pl.BlockSpec((tp, LANES), lambda b, i: (i, 0)),
                  full(k_norm)],
        out_specs=[pl.BlockSpec((1, GQA_KV_HEADS, LANES, tp), lambda b, i: (b, 0, 0, i)),
                   pl.BlockSpec((1, GQA_KV_HEADS, tp, LANES), lambda b, i: (b, 0, i, 0)),
                   pl.BlockSpec((1, 1, SUBLANES, LANES), lambda b, i: (b, i, 0, 0))],
        out_shape=[jax.ShapeDtypeStruct((B, GQA_KV_HEADS, LANES, T), BF16),
                   jax.ShapeDtypeStruct((B, GQA_KV_HEADS, T, LANES), BF16),
                   jax.ShapeDtypeStruct((B, nt, SUBLANES, LANES), F32)],
        compiler_params=_params(("parallel", "parallel")),
        name="gqa_prep_k",
    )(g1, cos, sin, k_norm)


def _gqa_prep_q_kernel(g_ref, cos_ref, sin_ref, qn_ref, kmax_ref, q_out, mb_out):
    x = g_ref[0]
    cos = cos_ref[...]
    sin = sin_ref[...]
    q = _head_rms(x[:, :GQA_W], qn_ref[...])
    q_scale = HEAD_DIM ** -0.5 * LOG2E
    kmax = kmax_ref[0]
    lane = lax.broadcasted_iota(jnp.int32, (x.shape[0], LANES), 1)
    first = lane < HEAD_DIM
    tiles = []
    mb_max = jnp.zeros((1, LANES), F32)
    for j in range(GQA_W // LANES):
        pair = _rope_tile(q[:, j * LANES:(j + 1) * LANES], cos, sin) * q_scale
        norm = jnp.sqrt(_head_sumsq(pair))
        for e in range(2):
            h = 2 * j + e
            xh = pair if e == 0 else pltpu.roll(pair, HEAD_DIM, 1)
            nh = norm if e == 1 else pltpu.roll(norm, HEAD_DIM, 1)
            mb = nh * kmax[:, h * LANES:(h + 1) * LANES] * GQA_BOUND_SLACK
            tiles.append(jnp.where(first, xh, jnp.where(lane == HEAD_DIM, -mb, 0.0)).astype(BF16))
            mb_max = jnp.maximum(mb_max, jnp.max(jnp.where(first, 0.0, mb), axis=0, keepdims=True))
    q_out[0] = jnp.concatenate(tiles, axis=1)
    mb_out[0, 0] = jnp.broadcast_to(mb_max, (SUBLANES, LANES))


def _gqa_prep_q(g1, cos, sin, q_norm, kmax, tp=512):
    B, T, _ = g1.shape
    nt = T // tp
    full = lambda a: pl.BlockSpec(a.shape, lambda b, i: (0, 0))
    return pl.pallas_call(
        _gqa_prep_q_kernel,
        grid=(B, nt),
        in_specs=[pl.BlockSpec((1, tp, G1_W), lambda b, i: (b, i, 0)),
                  pl.BlockSpec((tp, LANES), lambda b, i: (i, 0)),
                  pl.BlockSpec((tp, LANES), lambda b, i: (i, 0)),
                  full(q_norm),
                  pl.BlockSpec((1, 1, GQA_AUG_W), lambda b, i: (b, 0, 0))],
        out_specs=[pl.BlockSpec((1, tp, GQA_AUG_W), lambda b, i: (b, i, 0)),
                   pl.BlockSpec((1, 1, SUBLANES, LANES), lambda b, i: (b, i, 0, 0))],
        out_shape=[jax.ShapeDtypeStruct((B, T, GQA_AUG_W), BF16),
                   jax.ShapeDtypeStruct((B, nt, SUBLANES, LANES), F32)],
        compiler_params=_params(("parallel", "parallel")),
        name="gqa_prep_q",
    )(g1, cos, sin, q_norm, kmax)


def _gqa_store_heads(o_ref, o, kh, tq):
    for kk in range(GQA_KV_HEADS):
        @pl.when(kh == kk)
        def _():
            for h in range(GQA_GROUP):
                lo = (kk * GQA_GROUP + h) * HEAD_DIM
                o_ref[0, :, lo:lo + HEAD_DIM] = o[h * tq:(h + 1) * tq, :HEAD_DIM].astype(o_ref.dtype)


def _gqa_fast_kernel(q_ref, kt_ref, v_ref, o_ref, acc_ref, *, tq, tk, kc, nk):
    kh = pl.program_id(2)
    ki = pl.program_id(3)

    @pl.when(ki == 0)
    def _():
        acc_ref[...] = jnp.zeros_like(acc_ref)

    qs = jnp.concatenate([q_ref[0, :, h * LANES:(h + 1) * LANES] for h in range(GQA_GROUP)], axis=0)
    acc = None
    for c in range(tk // kc):
        s = jnp.dot(qs, kt_ref[0, 0, :, c * kc:(c + 1) * kc], preferred_element_type=F32)
        p = jnp.exp2(s).astype(BF16)
        d = jnp.dot(p, v_ref[0, 0, c * kc:(c + 1) * kc, :], preferred_element_type=F32)
        acc = d if acc is None else acc + d
    acc_ref[...] += acc

    @pl.when(ki == nk - 1)
    def _():
        a = acc_ref[...]
        _gqa_store_heads(o_ref, a / pltpu.roll(a, HEAD_DIM, 1), kh, tq)


def _gqa_online_kernel(q_ref, kt_ref, v_ref, o_ref, m_ref, acc_ref, *, tq, nk):
    kh = pl.program_id(2)
    ki = pl.program_id(3)

    @pl.when(ki == 0)
    def _():
        m_ref[...] = jnp.full_like(m_ref, NEG_INF)
        acc_ref[...] = jnp.zeros_like(acc_ref)

    qs = jnp.concatenate([q_ref[0, :, h * LANES:h * LANES + HEAD_DIM] for h in range(GQA_GROUP)], axis=0)
    s = jnp.dot(qs, kt_ref[0, 0, :HEAD_DIM, :], preferred_element_type=F32)
    m_prev = m_ref[...]
    m_new = jnp.maximum(m_prev, jnp.max(s, axis=-1, keepdims=True))
    alpha = jnp.exp2(m_prev - m_new)
    p = jnp.exp2(s - m_new[:, :1]).astype(BF16)
    acc_ref[...] = alpha * acc_ref[...] + jnp.dot(p, v_ref[0, 0], preferred_element_type=F32)
    m_ref[...] = m_new

    @pl.when(ki == nk - 1)
    def _():
        a = acc_ref[...]
        _gqa_store_heads(o_ref, a / pltpu.roll(a, HEAD_DIM, 1), kh, tq)


def _gqa_attend(q, kt, v, fast, tq=256):
    B, T, _ = q.shape
    tk = min(2048, T) if fast else 512
    nq, nk = T // tq, T // tk
    m_rows = GQA_GROUP * tq
    acc = pltpu.VMEM((m_rows, LANES), F32)
    if fast:
        body = functools.partial(_gqa_fast_kernel, tq=tq, tk=tk, kc=512, nk=nk)
        scratch = [acc]
    else:
        body = functools.partial(_gqa_online_kernel, tq=tq, nk=nk)
        scratch = [acc, acc]
    return pl.pallas_call(
        body,
        grid=(B, nq, GQA_KV_HEADS, nk),
        in_specs=[pl.BlockSpec((1, tq, GQA_GROUP * LANES), lambda b, i, h, j: (b, i, h)),
                  pl.BlockSpec((1, 1, LANES, tk), lambda b, i, h, j: (b, h, 0, j)),
                  pl.BlockSpec((1, 1, tk, LANES), lambda b, i, h, j: (b, h, j, 0))],
        out_specs=pl.BlockSpec((1, tq, GQA_W), lambda b, i, h, j: (b, i, 0)),
        out_shape=jax.ShapeDtypeStruct((B, T, GQA_W), BF16),
        scratch_shapes=scratch,
        compiler_params=_params(("parallel", "arbitrary", "arbitrary", "arbitrary")),
        name="gqa_fast" if fast else "gqa_online",
    )(q, kt, v)


def _gqa(g1, cos, sin, q_norm, k_norm):
    B = g1.shape[0]
    kt, v, kss = _gqa_prep_k(g1, cos, sin, k_norm)
    knorm = jnp.sqrt(jnp.max(kss, axis=(1, 2)))[:, ::HEAD_DIM]
    kmax = jnp.repeat(knorm, GQA_GROUP * LANES, axis=1).reshape(B, 1, GQA_AUG_W)
    q, mb = _gqa_prep_q(g1, cos, sin, q_norm, kmax)
    safe = jnp.max(mb) * (1.0 + 2.0 ** -8) <= GQA_SAFE_LOG2
    return lax.cond(safe,
                    functools.partial(_gqa_attend, fast=True),
                    functools.partial(_gqa_attend, fast=False), q, kt, v)


def _dil_kernel(q_ref, k_ref, v_ref, o_ref, lse_ref, *, S, bq, kw, dilation, slopes):
    p0 = pl.program_id(2) * bq
    start = pl.multiple_of(jnp.clip(p0 - DIL_SIDE, 0, S - kw), DIL_SIDE)
    q = q_ref[0, 0]
    kwin = k_ref[0, 0, pl.ds(start, kw), :]
    vwin = v_ref[0, 0, pl.ds(start, kw), :]
    qpos = p0 + lax.broadcasted_iota(jnp.int32, (bq, kw), 0)
    kpos = start + lax.broadcasted_iota(jnp.int32, (bq, kw), 1)
    rel = jnp.abs(kpos - qpos)
    valid = rel <= DIL_SIDE
    dist = (rel * dilation).astype(F32)
    hid = _head_id((bq, LANES), 1)
    outs, lses = [], []
    for hh in range(DIL_HEADS_PER_GROUP):
        qm = jnp.where(hid == hh, q, jnp.zeros_like(q))
        s = lax.dot_general(qm, kwin, (((1,), (1,)), ((), ())), preferred_element_type=F32)
        s = jnp.where(valid, s * HEAD_DIM ** -0.5 - slopes[hh] * dist, NEG_INF)
        m = jnp.max(s, axis=-1, keepdims=True)
        p = jnp.exp(s - m)
        l = jnp.sum(p, axis=-1, keepdims=True)
        outs.append(jnp.dot(p.astype(BF16), vwin, preferred_element_type=F32) / l)
        lses.append(m + jnp.log(l))
    o_ref[0, 0] = jnp.where(hid == 0, outs[0], outs[1])
    lse_ref[0, 0] = jnp.where(hid == 0, lses[0], lses[1])


def _dilated(q, k, v, dilation, slopes):
    B, d, S, _ = q.shape
    bq = min(128, S)
    kw = min(bq + 2 * DIL_SIDE, S)
    seq = pl.BlockSpec((1, 1, S, LANES), lambda b, r, i: (b, r, 0, 0))
    blk = pl.BlockSpec((1, 1, bq, LANES), lambda b, r, i: (b, r, i, 0))
    return pl.pallas_call(
        functools.partial(_dil_kernel, S=S, bq=bq, kw=kw, dilation=dilation, slopes=slopes),
        grid=(B, d, S // bq),
        in_specs=[blk, seq, seq],
        out_specs=[blk, blk],
        out_shape=[jax.ShapeDtypeStruct((B, d, S, LANES), F32)] * 2,
        compiler_params=_params(("parallel", "parallel", "arbitrary")),
        name=f"dilated_{dilation}",
    )(q, k, v)


def _outproj_kernel(x_ref, hf_ref, hb_ref, z_ref, b_ref, o0_ref, o1_ref, o2_ref, l0_ref, l1_ref, l2_ref,
                    gain_ref, w_ref, gpost_ref, out_ref):
    z = z_ref[...]
    mo = z[:, :MLSTM_W]
    mz = z[:, MLSTM_W:2 * MLSTM_W]
    gz = z[:, 2 * MLSTM_W:2 * MLSTM_W + GQA_W]
    dz = z[:, 2 * MLSTM_W + GQA_W:]

    h = _sigmoid(mo) * (hf_ref[...] + hb_ref[...])
    same_head = (_head_id((MLSTM_W, MLSTM_W), 0) == _head_id((MLSTM_W, MLSTM_W), 1)).astype(F32)
    ss = jnp.dot(h * h, same_head, precision=HIGHEST, preferred_element_type=F32)
    a = h * lax.rsqrt(ss * (1.0 / HEAD_DIM) + NORM_EPS) * gain_ref[...] * _silu(mz)

    b = b_ref[...].astype(F32) * _silu(gz)

    lses = [l0_ref[...], l1_ref[...], l2_ref[...]]
    outs = [o0_ref[...], o1_ref[...], o2_ref[...]]
    mx = jnp.maximum(jnp.maximum(lses[0], lses[1]), lses[2])
    es = [jnp.exp(l - mx) for l in lses]
    inv = 1.0 / (es[0] + es[1] + es[2])
    c = jnp.concatenate([outs[g] * (es[g] * inv) for g in range(3)], axis=1) * _silu(dz)

    w = w_ref[...]
    y = jnp.dot(a.astype(BF16), w[:MLSTM_W], preferred_element_type=F32)
    y += jnp.dot(b.astype(BF16), w[MLSTM_W:MLSTM_W + GQA_W], preferred_element_type=F32)
    y += jnp.dot(c.astype(BF16), w[MLSTM_W + GQA_W:], preferred_element_type=F32)
    ms = jnp.mean(y * y, axis=-1, keepdims=True)
    out_ref[...] = x_ref[...] + y * lax.rsqrt(ms + NORM_EPS) * gpost_ref[...]


def _outproj(x, hf, hb, z, b, dil_o, dil_l, gain, w, gpost, tm=256):
    n = x.shape[0]
    row = lambda a: pl.BlockSpec((tm, a.shape[1]), lambda i: (i, 0))
    full = lambda a: pl.BlockSpec(a.shape, lambda i: (0, 0))
    rows = [x, hf, hb, z, b] + list(dil_o) + list(dil_l)
    consts = [gain, w, gpost]
    return pl.pallas_call(
        _outproj_kernel,
        grid=(n // tm,),
        in_specs=[row(a) for a in rows] + [full(a) for a in consts],
        out_specs=pl.BlockSpec((tm, D_MODEL), lambda i: (i, 0)),
        out_shape=jax.ShapeDtypeStruct((n, D_MODEL), F32),
        compiler_params=_params(("parallel",)),
        name="outproj",
    )(*rows, *consts)


def _rope_tables(T):
    rows = T // GRID_W
    r, c = jnp.meshgrid(jnp.arange(rows), jnp.arange(GRID_W), indexing="ij")
    r = r.reshape(-1).astype(F32)
    c = c.reshape(-1).astype(F32)
    axis_dim = HEAD_DIM // 2
    inv = ROPE_THETA ** (-jnp.arange(0, axis_dim, 2, dtype=F32) / axis_dim)
    ang = jnp.concatenate([r[:, None] * inv, c[:, None] * inv], axis=-1)
    cos, sin = jnp.cos(ang), jnp.sin(ang)
    cos_t = jnp.tile(jnp.concatenate([cos, cos], axis=-1), (1, LANES // HEAD_DIM))
    sin_t = jnp.tile(jnp.concatenate([-sin, sin], axis=-1), (1, LANES // HEAD_DIM))
    return cos_t, sin_t


def _split_w_in(w):
    cuts = np.cumsum((0,) + IN_SPLITS)
    mq, mk, mv, mo, mg, mz, gq, gk, gv, gz, dq, dk, dv, dz = [w[:, cuts[i]:cuts[i + 1]] for i in range(14)]
    pad = jnp.zeros((w.shape[0], GATE_PAD - mg.shape[1]), w.dtype)
    cat = lambda *xs: jnp.concatenate(xs, axis=1).astype(BF16)
    return cat(mq, mk, mv, mg, pad), cat(gq, gk, gv), cat(dq, dk, dv), cat(mo, mz, gz, dz)


def _alibi_slopes():
    return [float(np.exp2(np.float32(-8.0) * np.float32(i) / np.float32(DIL_HEADS)))
            for i in range(1, DIL_HEADS + 1)]


def _layer(x, B, T, p, cos, sin, slopes):
    n = B * T
    m1, g1, d1, z = _inproj(x, p["norm_pre"], *p["w_in"])
    hf, hb = _mlstm(m1.reshape(B, T, M1_W), p["conv_w"], p["conv_b"], p["gate_bias"])
    b = _gqa(g1.reshape(B, T, G1_W), cos, sin, p["q_norm"], p["k_norm"])

    d1 = d1.reshape(B, T, D1_W)
    dil_o, dil_l = [], []
    for g, (_, dilation) in enumerate(DIL_PATTERNS):
        S = T // dilation

        def classes(off):
            a = d1[:, :, off + g * LANES:off + (g + 1) * LANES].astype(BF16)
            return a.reshape(B, S, dilation, LANES).transpose(0, 2, 1, 3)

        o_g, l_g = _dilated(classes(0), classes(DIL_W), classes(2 * DIL_W), dilation,
                            slopes[g * DIL_HEADS_PER_GROUP:(g + 1) * DIL_HEADS_PER_GROUP])
        back = lambda a: a.transpose(0, 2, 1, 3).reshape(n, LANES)
        dil_o.append(back(o_g))
        dil_l.append(back(l_g))

    return _outproj(x, hf.reshape(n, MLSTM_W), hb.reshape(n, MLSTM_W), z, b.reshape(n, GQA_W),
                    dil_o, dil_l, p["out_gain"], p["w_out"], p["norm_post"])


def _trunk(x, layers):
    B, T, _ = x.shape
    cos, sin = _rope_tables(T)
    slopes = _alibi_slopes()
    x = x.reshape(B * T, D_MODEL)
    for p in layers:
        x = _layer(x, B, T, p, cos, sin, slopes)
    return x.reshape(B, T, D_MODEL)


def kernel(x_prompt, x_sample, norm_pre, w_in, mlstm_gate_bias, mlstm_conv_w, mlstm_conv_b, mlstm_out_gain,
           gqa_q_norm, gqa_k_norm, w_out, norm_post):
    depth = w_in.shape[0]
    row = lambda a: a.reshape(1, -1).astype(F32)
    layers = []
    for l in range(depth):
        gate_bias = jnp.concatenate(
            [mlstm_gate_bias[l].astype(F32), jnp.zeros((GATE_PAD - 4 * MLSTM_HEADS,), F32)]).reshape(1, GATE_PAD)
        layers.append(dict(
            norm_pre=row(norm_pre[l]),
            w_in=_split_w_in(w_in[l]),
            gate_bias=gate_bias,
            conv_w=mlstm_conv_w[l].astype(F32),
            conv_b=row(mlstm_conv_b[l]),
            out_gain=row(mlstm_out_gain[l]),
            q_norm=row(jnp.tile(gqa_q_norm[l], GQA_Q_HEADS)),
            k_norm=row(jnp.tile(gqa_k_norm[l], GQA_KV_HEADS)),
            w_out=w_out[l].astype(BF16),
            norm_post=row(norm_post[l]),
        ))
    return (_trunk(x_prompt, layers), _trunk(x_sample, layers))
```

```python
import functools
import math

import numpy as np
import jax
import jax.numpy as jnp
from jax import lax
from jax.experimental import pallas as pl
from jax.experimental.pallas import tpu as pltpu

F32 = jnp.float32
BF16 = jnp.bfloat16

D_MODEL = 1024
HEAD_DIM = 64
NORM_EPS = 1e-6
GRID_W = 64
ROPE_THETA = 10000.0
MLSTM_HEADS = 4
MLSTM_W = MLSTM_HEADS * HEAD_DIM
MLSTM_CHUNK = 128
GQA_Q_HEADS = 6
GQA_KV_HEADS = 2
GQA_GROUP = GQA_Q_HEADS // GQA_KV_HEADS
GQA_W = GQA_Q_HEADS * HEAD_DIM
GQA_KV_W = GQA_KV_HEADS * HEAD_DIM
DIL_PATTERNS = ((128, 1), (512, 4), (2048, 16))
DIL_HEADS_PER_GROUP = 2
DIL_HEADS = DIL_HEADS_PER_GROUP * len(DIL_PATTERNS)
DIL_W = DIL_HEADS * HEAD_DIM
DIL_SIDE = 64
IN_SPLITS = (MLSTM_W, MLSTM_W, MLSTM_W, MLSTM_W, 4 * MLSTM_HEADS, MLSTM_W,
             GQA_W, GQA_KV_W, GQA_KV_W, GQA_W, DIL_W, DIL_W, DIL_W, DIL_W)

LANES = 128
SUBLANES = 8
GATE_PAD = LANES
M1_W = 3 * MLSTM_W + GATE_PAD
G1_W = GQA_W + 2 * GQA_KV_W
D1_W = 3 * DIL_W
Z_W = 2 * MLSTM_W + GQA_W + DIL_W
VMEM_LIMIT = 48 * 1024 * 1024
LOG2E = math.log2(math.e)
NEG_INF = float("-inf")


def _sigmoid(x):
    return 1.0 / (1.0 + jnp.exp(-x))


def _silu(x):
    return x * _sigmoid(x)


def _log_sigmoid(x):
    return jnp.minimum(x, 0.0) - jnp.log(1.0 + jnp.exp(-jnp.abs(x)))


def _head_id(shape, axis):
    return lax.broadcasted_iota(jnp.int32, shape, axis) // HEAD_DIM


def _params(sem):
    return pltpu.CompilerParams(dimension_semantics=sem, vmem_limit_bytes=VMEM_LIMIT)


def _split_terms(a, terms):
    out = []
    for t in range(terms):
        hi = a.astype(BF16)
        out.append(hi)
        if t + 1 < terms:
            a = a - hi.astype(F32)
    return out


def _inproj_kernel(x_ref, g_ref, wm_ref, wg_ref, wd_ref, wz_ref, om_ref, og_ref, od1_ref, od4_ref, od16_ref,
                   oz_ref, ds_ref, *, tm):
    x = x_ref[...]
    ms = jnp.mean(x * x, axis=-1, keepdims=True)
    h = (x * lax.rsqrt(ms + NORM_EPS) * g_ref[...]).astype(BF16)
    om_ref[...] = jnp.dot(h, wm_ref[...], preferred_element_type=F32)
    og_ref[...] = jnp.dot(h, wg_ref[...], preferred_element_type=F32)
    oz_ref[...] = jnp.dot(h, wz_ref[...], preferred_element_type=F32).astype(BF16)
    yd = jnp.dot(h, wd_ref[...], preferred_element_type=F32)
    n_tiles = D1_W // LANES
    for j in range(n_tiles):
        ds_ref[j] = yd[:, j * LANES:(j + 1) * LANES]
    for g, (out_ref, (_, d)) in enumerate(zip((od1_ref, od4_ref, od16_ref), DIL_PATTERNS)):
        for part in range(3):
            src = part * len(DIL_PATTERNS) + g
            for r in range(d):
                rows = ds_ref[src] if d == 1 else ds_ref[src, pl.ds(r, tm // d, stride=d), :]
                out_ref[0, r, :, part * LANES:(part + 1) * LANES] = rows.astype(BF16)


def _inproj(x, g, wm, wg, wd, wz, B, T, tm=256):
    n = x.shape[0]
    tps = T // tm
    row = lambda w: pl.BlockSpec((tm, w), lambda i: (i, 0))
    full = lambda a: pl.BlockSpec(a.shape, lambda i: (0, 0))
    dil_spec = lambda d: pl.BlockSpec((1, d, tm // d, 3 * LANES), lambda i: (i // tps, 0, i % tps, 0))
    dil_shape = lambda d: jax.ShapeDtypeStruct((B, d, T // d, 3 * LANES), BF16)
    dils = [d for _, d in DIL_PATTERNS]
    return pl.pallas_call(
        functools.partial(_inproj_kernel, tm=tm),
        grid=(n // tm,),
        in_specs=[row(D_MODEL), full(g), full(wm), full(wg), full(wd), full(wz)],
        out_specs=[row(M1_W), row(G1_W)] + [dil_spec(d) for d in dils] + [row(Z_W)],
        out_shape=[jax.ShapeDtypeStruct((n, M1_W), F32), jax.ShapeDtypeStruct((n, G1_W), F32)]
        + [dil_shape(d) for d in dils] + [jax.ShapeDtypeStruct((n, Z_W), BF16)],
        scratch_shapes=[pltpu.VMEM((D1_W // LANES, tm, LANES), F32)],
        compiler_params=_params(("parallel",)),
        name="inproj",
    )(x, g, wm, wg, wd, wz)


def _expand_heads(cols, rows):
    hid = _head_id((rows, MLSTM_W), 1)
    out = jnp.broadcast_to(cols[MLSTM_HEADS - 1], (rows, MLSTM_W))
    for h in range(MLSTM_HEADS - 2, -1, -1):
        out = jnp.where(hid == h, cols[h], out)
    return out


def _mlstm_direction(x_ref, prev_ref, next_ref, out_ref, c_ref, n_ref, m_ref, cw_ref, cb_ref, gb_ref,
                     prev_valid, next_valid, d, forward):
    L = MLSTM_CHUNK
    W = MLSTM_W
    x = x_ref[0]
    qk = x[:, :2 * W]
    prev_row = prev_ref[0][SUBLANES - 1:SUBLANES, :] * prev_valid
    next_row = next_ref[0][0:1, :] * next_valid
    rid = lax.broadcasted_iota(jnp.int32, (L, 2 * W), 0)
    qk_m1 = jnp.where(rid == 0, prev_row, pltpu.roll(qk, 1, 0))
    qk_p1 = jnp.where(rid == L - 1, next_row, pltpu.roll(qk, L - 1, 0))
    cw = cw_ref[...]
    conv = qk_m1 * cw[0:1, :] + qk * cw[1:2, :] + qk_p1 * cw[2:3, :] + cb_ref[...]
    qk_act = _silu(conv)
    q = qk_act[:, :W]
    k = qk_act[:, W:] * HEAD_DIM ** -0.5
    v = x[:, 2 * W:3 * W]
    g = x[:, 3 * W:] + gb_ref[...]

    io = 0 if forward else 2 * MLSTM_HEADS
    fo = io + MLSTM_HEADS
    r_i = lax.broadcasted_iota(jnp.int32, (L, L), 0)
    c_i = lax.broadcasted_iota(jnp.int32, (L, L), 1)
    seen = (c_i <= r_i) if forward else (c_i >= r_i)
    seen01 = jnp.where(seen, 1.0, 0.0).astype(BF16)
    cum = sum(jnp.dot(seen01, t, preferred_element_type=F32)
              for t in _split_terms(_log_sigmoid(g), 3))
    gl = lax.broadcasted_iota(jnp.int32, (L, GATE_PAD), 1)
    rows_t = jnp.where((gl >= fo) & (gl < fo + MLSTM_HEADS), cum, g).T
    tot_row = L - 1 if forward else 0

    hid = _head_id((L, W), 1)
    qb = q.astype(BF16)
    kb = k.astype(BF16)
    m_state = m_ref[d][0:1, :]
    w_blocks, m_rows, w_inters, k_facs, decays, m_news = [], [], [], [], [], []
    for h in range(MLSTM_HEADS):
        bc = cum[:, fo + h:fo + h + 1]
        ic = g[:, io + h:io + h + 1]
        row_term = rows_t[io + h:io + h + 1, :] - rows_t[fo + h:fo + h + 1, :]
        dmat = jnp.where(seen, bc + row_term, NEG_INF)
        m_h = m_state[:, h * HEAD_DIM:h * HEAD_DIM + 1]
        d_inter = bc + m_h
        m_row = jnp.maximum(d_inter, jnp.max(dmat, axis=1, keepdims=True))
        s_h = lax.dot_general(jnp.where(hid == h, qb, jnp.zeros_like(qb)), kb,
                              (((1,), (1,)), ((), ())), preferred_element_type=F32)
        w_blocks.append((jnp.exp(dmat - m_row) * s_h).astype(BF16))
        b_tot = cum[tot_row:tot_row + 1, fo + h:fo + h + 1]
        gv = b_tot - bc + ic
        m_new = jnp.maximum(b_tot + m_h, jnp.max(gv, axis=0, keepdims=True))
        m_rows.append(m_row)
        w_inters.append(jnp.exp(d_inter - m_row))
        k_facs.append(jnp.exp(gv - m_new))
        decays.append(jnp.exp(b_tot + m_h - m_new))
        m_news.append(m_new)

    m_row_e = _expand_heads(m_rows, L)
    w_inter_e = _expand_heads(w_inters, L)
    k_fac_e = _expand_heads(k_facs, L)
    decay_e = _expand_heads(decays, 1)
    m_new_e = _expand_heads(m_news, 1)

    w_cat = jnp.concatenate(w_blocks, axis=1)
    v_stack = jnp.concatenate([jnp.where(hid == h, v, 0.0) for h in range(MLSTM_HEADS)],
                              axis=0).astype(BF16)
    sr = lax.broadcasted_iota(jnp.int32, (MLSTM_HEADS * L, W), 0) // L
    ones_stack = (sr == _head_id((MLSTM_HEADS * L, W), 1)).astype(BF16)
    same_head = _head_id((W, W), 0) == _head_id((W, W), 1)

    c_state = c_ref[d]
    n_state = n_ref[d][0:1, :]
    q_c = jnp.dot(qb, c_state.astype(BF16), preferred_element_type=F32)
    q_n = jnp.dot((q * n_state).astype(BF16), same_head.astype(BF16), preferred_element_type=F32)
    num = w_inter_e * q_c + jnp.dot(w_cat, v_stack, preferred_element_type=F32)
    den = w_inter_e * q_n + jnp.dot(w_cat, ones_stack, preferred_element_type=F32)
    out_ref[0] = num / jnp.maximum(jnp.abs(den), jnp.exp(-m_row_e))

    kw = k_fac_e * k
    kv = lax.dot_general(kw.astype(BF16), v.astype(BF16), (((0,), (0,)), ((), ())),
                         preferred_element_type=F32)
    c_ref[d] = decay_e * c_state + jnp.where(same_head, kv, 0.0)
    n_new = decay_e * n_state + jnp.sum(kw, axis=0, keepdims=True)
    n_ref[d] = jnp.broadcast_to(n_new, (SUBLANES, W))
    m_ref[d] = jnp.broadcast_to(m_new_e, (SUBLANES, W))


def _mlstm_kernel(xf_ref, pf_ref, nf_ref, xb_ref, pb_ref, nb_ref, cw_ref, cb_ref, gb_ref,
                  hf_ref, hb_ref, c_ref, n_ref, m_ref, *, nc):
    c = pl.program_id(1)

    @pl.when(c == 0)
    def _():
        c_ref[...] = jnp.zeros_like(c_ref)
        n_ref[...] = jnp.zeros_like(n_ref)
        m_ref[...] = jnp.zeros_like(m_ref)

    not_first = jnp.where(c > 0, 1.0, 0.0).astype(F32)
    not_last = jnp.where(c < nc - 1, 1.0, 0.0).astype(F32)
    _mlstm_direction(xf_ref, pf_ref, nf_ref, hf_ref, c_ref, n_ref, m_ref, cw_ref, cb_ref, gb_ref,
                     not_first, not_last, 0, True)
    _mlstm_direction(xb_ref, pb_ref, nb_ref, hb_ref, c_ref, n_ref, m_ref, cw_ref, cb_ref, gb_ref,
                     not_last, not_first, 1, False)


def _mlstm(m1, conv_w, conv_b, gate_bias):
    B, T, _ = m1.shape
    L = MLSTM_CHUNK
    nc = T // L
    hb = L // SUBLANES
    last_halo = T // SUBLANES - 1
    qkw = 2 * MLSTM_W
    fwd = lambda c: c
    bwd = lambda c: nc - 1 - c

    def specs(pos):
        return [
            pl.BlockSpec((1, L, M1_W), lambda b, c: (b, pos(c), 0)),
            pl.BlockSpec((1, SUBLANES, qkw), lambda b, c: (b, jnp.maximum(pos(c) * hb - 1, 0), 0)),
            pl.BlockSpec((1, SUBLANES, qkw), lambda b, c: (b, jnp.minimum((pos(c) + 1) * hb, last_halo), 0)),
        ]

    full = lambda a: pl.BlockSpec(a.shape, lambda b, c: (0, 0))
    return pl.pallas_call(
        functools.partial(_mlstm_kernel, nc=nc),
        grid=(B, nc),
        in_specs=specs(fwd) + specs(bwd) + [full(conv_w), full(conv_b), full(gate_bias)],
        out_specs=[pl.BlockSpec((1, L, MLSTM_W), lambda b, c: (b, c, 0)),
                   pl.BlockSpec((1, L, MLSTM_W), lambda b, c: (b, nc - 1 - c, 0))],
        out_shape=[jax.ShapeDtypeStruct((B, T, MLSTM_W), F32)] * 2,
        scratch_shapes=[pltpu.VMEM((2, MLSTM_W, MLSTM_W), F32),
                        pltpu.VMEM((2, SUBLANES, MLSTM_W), F32),
                        pltpu.VMEM((2, SUBLANES, MLSTM_W), F32)],
        compiler_params=_params(("parallel", "arbitrary")),
        name="mlstm",
    )(m1, m1, m1, m1, m1, m1, conv_w, conv_b, gate_bias)


GQA_SAFE_LOG2 = 56.0
GQA_BOUND_SLACK = 1.0 + 2.0 ** -6
GQA_AUG_W = GQA_Q_HEADS * LANES
GQA_SCALE = HEAD_DIM ** -0.5 * LOG2E


def _rope_tile(a, cos, sin):
    half = HEAD_DIM // 2
    lane = lax.broadcasted_iota(jnp.int32, a.shape, 1) % HEAD_DIM
    rot = jnp.where(lane < half, pltpu.roll(a, LANES - half, 1), pltpu.roll(a, half, 1))
    return a * cos + rot * sin


def _head_sumsq(a):
    w = a.shape[1]
    same_head = jnp.where(_head_id((w, w), 0) == _head_id((w, w), 1), 1.0, 0.0).astype(BF16)
    return sum(jnp.dot(t, same_head, preferred_element_type=F32) for t in _split_terms(a * a, 2))


def _head_rms(a, gain):
    return a * lax.rsqrt(_head_sumsq(a) * (1.0 / HEAD_DIM) + NORM_EPS) * gain


def _gqa_prep_kernel(g_ref, cos_ref, sin_ref, qn_ref, kn_ref, nmb_ref, q_out, kt_out, v_out):
    x = g_ref[0]
    cos = cos_ref[...]
    sin = sin_ref[...]
    q = _head_rms(x[:, :GQA_W], qn_ref[...])
    k = _head_rms(x[:, GQA_W:GQA_W + GQA_KV_W], kn_ref[...])
    v = x[:, GQA_W + GQA_KV_W:]
    lane = lax.broadcasted_iota(jnp.int32, (x.shape[0], LANES), 1)
    first = lane < HEAD_DIM
    bound_col = jnp.where(lane == HEAD_DIM, nmb_ref[...], 0.0)
    ones_col = jnp.where(lane == HEAD_DIM, 1.0, 0.0)
    tiles = []
    for j in range(GQA_W // LANES):
        pair = _rope_tile(q[:, j * LANES:(j + 1) * LANES], cos, sin) * GQA_SCALE
        tiles.append(jnp.where(first, pair, bound_col).astype(BF16))
        tiles.append(jnp.where(first, pltpu.roll(pair, HEAD_DIM, 1), bound_col).astype(BF16))
    q_out[0] = jnp.concatenate(tiles, axis=1)
    k_rot = _rope_tile(k, cos, sin)
    kt_out[0, 0] = jnp.where(first, k_rot, ones_col).T.astype(BF16)
    kt_out[0, 1] = jnp.where(first, pltpu.roll(k_rot, HEAD_DIM, 1), ones_col).T.astype(BF16)
    v_out[0, 0] = jnp.where(first, v, 1.0).astype(BF16)
    v_out[0, 1] = jnp.where(first, pltpu.roll(v, HEAD_DIM, 1), 1.0).astype(BF16)


def _gqa_prep(g1, cos, sin, q_norm, k_norm, neg_mb, tp=512):
    B, T, _ = g1.shape
    full = lambda a: pl.BlockSpec(a.shape, lambda b, i: (0, 0))
    return pl.pallas_call(
        _gqa_prep_kernel,
        grid=(B, T // tp),
        in_specs=[pl.BlockSpec((1, tp, G1_W), lambda b, i: (b, i, 0)),
                  pl.BlockSpec((tp, LANES), lambda b, i: (i, 0)),
                  pl.BlockSpec((tp, LANES), lambda b, i: (i, 0)),
                  full(q_norm), full(k_norm), full(neg_mb)],
        out_specs=[pl.BlockSpec((1, tp, GQA_AUG_W), lambda b, i: (b, i, 0)),
                   pl.BlockSpec((1, GQA_KV_HEADS, LANES, tp), lambda b, i: (b, 0, 0, i)),
                   pl.BlockSpec((1, GQA_KV_HEADS, tp, LANES), lambda b, i: (b, 0, i, 0))],
        out_shape=[jax.ShapeDtypeStruct((B, T, GQA_AUG_W), BF16),
                   jax.ShapeDtypeStruct((B, GQA_KV_HEADS, LANES, T), BF16),
                   jax.ShapeDtypeStruct((B, GQA_KV_HEADS, T, LANES), BF16)],
        compiler_params=_params(("parallel", "parallel")),
        name="gqa_prep",
    )(g1, cos, sin, q_norm, k_norm, neg_mb)


def _gqa_store_heads(o_ref, o, kh, tq):
    for kk in range(GQA_KV_HEADS):
        @pl.when(kh == kk)
        def _():
            for h in range(GQA_GROUP):
                lo = (kk * GQA_GROUP + h) * HEAD_DIM
                o_ref[0, :, lo:lo + HEAD_DIM] = o[h * tq:(h + 1) * tq, :HEAD_DIM].astype(o_ref.dtype)


def _gqa_fast_kernel(q_ref, kt_ref, v_ref, o_ref, acc_ref, *, tq, tk, kc, nk):
    kh = pl.program_id(2)
    ki = pl.program_id(3)

    @pl.when(ki == 0)
    def _():
        acc_ref[...] = jnp.zeros_like(acc_ref)

    qs = jnp.concatenate([q_ref[0, :, h * LANES:(h + 1) * LANES] for h in range(GQA_GROUP)], axis=0)
    acc = None
    for c in range(tk // kc):
        s = jnp.dot(qs, kt_ref[0, 0, :, c * kc:(c + 1) * kc], preferred_element_type=F32)
        p = jnp.exp2(s).astype(BF16)
        d = jnp.dot(p, v_ref[0, 0, c * kc:(c + 1) * kc, :], preferred_element_type=F32)
        acc = d if acc is None else acc + d
    acc_ref[...] += acc

    @pl.when(ki == nk - 1)
    def _():
        a = acc_ref[...]
        _gqa_store_heads(o_ref, a / pltpu.roll(a, HEAD_DIM, 1), kh, tq)


def _gqa_online_kernel(q_ref, kt_ref, v_ref, o_ref, m_ref, acc_ref, *, tq, nk):
    kh = pl.program_id(2)
    ki = pl.program_id(3)

    @pl.when(ki == 0)
    def _():
        m_ref[...] = jnp.full_like(m_ref, NEG_INF)
        acc_ref[...] = jnp.zeros_like(acc_ref)

    qs = jnp.concatenate([q_ref[0, :, h * LANES:h * LANES + HEAD_DIM] for h in range(GQA_GROUP)], axis=0)
    s = jnp.dot(qs, kt_ref[0, 0, :HEAD_DIM, :], preferred_element_type=F32)
    m_prev = m_ref[...]
    m_new = jnp.maximum(m_prev, jnp.max(s, axis=-1, keepdims=True))
    alpha = jnp.exp2(m_prev - m_new)
    p = jnp.exp2(s - m_new[:, :1]).astype(BF16)
    acc_ref[...] = alpha * acc_ref[...] + jnp.dot(p, v_ref[0, 0], preferred_element_type=F32)
    m_ref[...] = m_new

    @pl.when(ki == nk - 1)
    def _():
        a = acc_ref[...]
        _gqa_store_heads(o_ref, a / pltpu.roll(a, HEAD_DIM, 1), kh, tq)


def _gqa_attend(q, kt, v, fast, tq=256):
    B, T, _ = q.shape
    tk = min(2048, T) if fast else 512
    nq, nk = T // tq, T // tk
    m_rows = GQA_GROUP * tq
    acc = pltpu.VMEM((m_rows, LANES), F32)
    if fast:
        body = functools.partial(_gqa_fast_kernel, tq=tq, tk=tk, kc=512, nk=nk)
        scratch = [acc]
    else:
        body = functools.partial(_gqa_online_kernel, tq=tq, nk=nk)
        scratch = [acc, acc]
    return pl.pallas_call(
        body,
        grid=(B, nq, GQA_KV_HEADS, nk),
        in_specs=[pl.BlockSpec((1, tq, GQA_GROUP * LANES), lambda b, i, h, j: (b, i, h)),
                  pl.BlockSpec((1, 1, LANES, tk), lambda b, i, h, j: (b, h, 0, j)),
                  pl.BlockSpec((1, 1, tk, LANES), lambda b, i, h, j: (b, h, j, 0))],
        out_specs=pl.BlockSpec((1, tq, GQA_W), lambda b, i, h, j: (b, i, 0)),
        out_shape=jax.ShapeDtypeStruct((B, T, GQA_W), BF16),
        scratch_shapes=scratch,
        compiler_params=_params(("parallel", "arbitrary", "arbitrary", "arbitrary")),
        name="gqa_fast" if fast else "gqa_online",
    )(q, kt, v)


def _gqa(g1, cos, sin, q_norm, k_norm):
    mb = (HEAD_DIM * GQA_SCALE * GQA_BOUND_SLACK) * jnp.max(jnp.abs(q_norm)) * jnp.max(jnp.abs(k_norm))
    q, kt, v = _gqa_prep(g1, cos, sin, q_norm, k_norm, jnp.full((1, LANES), -mb, F32))
    return lax.cond(mb <= GQA_SAFE_LOG2,
                    functools.partial(_gqa_attend, fast=True),
                    functools.partial(_gqa_attend, fast=False), q, kt, v)


DIL_UNITS = 8


def _dil_kernel(q_ref, k_ref, v_ref, o_ref, lse_ref, *, S, bq, kw, nb, rb, dilation, slopes):
    base = pl.program_id(2) * (nb * bq)
    row = lax.broadcasted_iota(jnp.int32, (bq, kw), 0)
    col = lax.broadcasted_iota(jnp.int32, (bq, kw), 1)
    hid = _head_id((bq, LANES), 1)
    for rr in range(rb):
        for n in range(nb):
            p0 = base + n * bq
            if kw == S:
                start = 0
                kwin, vwin = k_ref[0, rr], v_ref[0, rr]
            else:
                start = pl.multiple_of(jnp.clip(p0 - DIL_SIDE, 0, S - kw), DIL_SIDE)
                kwin = k_ref[0, rr, pl.ds(start, kw), :]
                vwin = v_ref[0, rr, pl.ds(start, kw), :]
            q = q_ref[0, rr, n * bq:(n + 1) * bq, :]
            rel = jnp.abs(col - row + (start - p0))
            valid = rel <= DIL_SIDE
            dist = (rel * dilation).astype(F32)
            outs, lses = [], []
            for hh in range(DIL_HEADS_PER_GROUP):
                qm = jnp.where(hid == hh, q, jnp.zeros_like(q))
                s = lax.dot_general(qm, kwin, (((1,), (1,)), ((), ())), preferred_element_type=F32)
                s = jnp.where(valid, s * HEAD_DIM ** -0.5 - slopes[hh] * dist, NEG_INF)
                m = jnp.max(s, axis=-1, keepdims=True)
                p = jnp.exp(s - m)
                l = jnp.sum(p, axis=-1, keepdims=True)
                outs.append(jnp.dot(p.astype(BF16), vwin, preferred_element_type=F32) / l)
                lses.append(m + jnp.log(l))
            o_ref[0, rr, n * bq:(n + 1) * bq, :] = jnp.where(hid == 0, outs[0], outs[1]).astype(o_ref.dtype)
            lse_ref[0, rr, n * bq:(n + 1) * bq, :] = jnp.where(hid == 0, lses[0], lses[1])


def _dilated(qkv, dilation, slopes):
    B, d, S, _ = qkv.shape
    bq = min(128, S)
    kw = min(bq + 2 * DIL_SIDE, S)
    nb = min(DIL_UNITS, S // bq)
    rb = min(d, DIL_UNITS // nb)
    seq = lambda part: pl.BlockSpec((1, rb, S, LANES), lambda b, r, i: (b, r, 0, part))
    blk = lambda part: pl.BlockSpec((1, rb, nb * bq, LANES), lambda b, r, i: (b, r, i, part))
    return pl.pallas_call(
        functools.partial(_dil_kernel, S=S, bq=bq, kw=kw, nb=nb, rb=rb, dilation=dilation, slopes=slopes),
        grid=(B, d // rb, S // (nb * bq)),
        in_specs=[blk(0), seq(1), seq(2)],
        out_specs=[blk(0), blk(0)],
        out_shape=[jax.ShapeDtypeStruct((B, d, S, LANES), BF16), jax.ShapeDtypeStruct((B, d, S, LANES), F32)],
        compiler_params=_params(("parallel", "parallel", "arbitrary")),
        name=f"dilated_{dilation}",
    )(qkv, qkv, qkv)


def _outproj_kernel(x_ref, hf_ref, hb_ref, z_ref, b_ref, o0_ref, o1_ref, o2_ref, l0_ref, l1_ref, l2_ref,
                    gain_ref, w_ref, gpost_ref, out_ref, nat_ref, *, tm):
    z = z_ref[...].astype(F32)
    mo = z[:, :MLSTM_W]
    mz = z[:, MLSTM_W:2 * MLSTM_W]
    gz = z[:, 2 * MLSTM_W:2 * MLSTM_W + GQA_W]
    dz = z[:, 2 * MLSTM_W + GQA_W:]

    h = _sigmoid(mo) * (hf_ref[...] + hb_ref[...])
    a = h * lax.rsqrt(_head_sumsq(h) * (1.0 / HEAD_DIM) + NORM_EPS) * gain_ref[...] * _silu(mz)

    b = b_ref[...].astype(F32) * _silu(gz)

    n_groups = len(DIL_PATTERNS)
    for g, (o_ref, l_ref, (_, d)) in enumerate(zip((o0_ref, o1_ref, o2_ref), (l0_ref, l1_ref, l2_ref), DIL_PATTERNS)):
        for r in range(d):
            dst = slice(None) if d == 1 else pl.ds(r, tm // d, stride=d)
            nat_ref[g, dst, :] = o_ref[0, r].astype(F32)
            nat_ref[n_groups + g, dst, :] = l_ref[0, r]
    outs = [nat_ref[g] for g in range(n_groups)]
    lses = [nat_ref[n_groups + g] for g in range(n_groups)]
    mx = jnp.maximum(jnp.maximum(lses[0], lses[1]), lses[2])
    es = [jnp.exp(l - mx) for l in lses]
    inv = 1.0 / (es[0] + es[1] + es[2])
    c = jnp.concatenate([outs[g] * (es[g] * inv) for g in range(3)], axis=1) * _silu(dz)

    w = w_ref[...]
    y = jnp.dot(a.astype(BF16), w[:MLSTM_W], preferred_element_type=F32)
    y += jnp.dot(b.astype(BF16), w[MLSTM_W:MLSTM_W + GQA_W], preferred_element_type=F32)
    y += jnp.dot(c.astype(BF16), w[MLSTM_W + GQA_W:], preferred_element_type=F32)
    ms = jnp.mean(y * y, axis=-1, keepdims=True)
    out_ref[...] = x_ref[...] + y * lax.rsqrt(ms + NORM_EPS) * gpost_ref[...]


def _outproj(x, hf, hb, z, b, dil_o, dil_l, gain, w, gpost, T, tm=256):
    n = x.shape[0]
    tps = T // tm
    row = lambda a: pl.BlockSpec((tm, a.shape[1]), lambda i: (i, 0))
    full = lambda a: pl.BlockSpec(a.shape, lambda i: (0, 0))
    cls = lambda a: pl.BlockSpec((1, a.shape[1], tm // a.shape[1], LANES), lambda i: (i // tps, 0, i % tps, 0))
    rows = [x, hf, hb, z, b]
    dil = list(dil_o) + list(dil_l)
    consts = [gain, w, gpost]
    return pl.pallas_call(
        functools.partial(_outproj_kernel, tm=tm),
        grid=(n // tm,),
        in_specs=[row(a) for a in rows] + [cls(a) for a in dil] + [full(a) for a in consts],
        out_specs=pl.BlockSpec((tm, D_MODEL), lambda i: (i, 0)),
        out_shape=jax.ShapeDtypeStruct((n, D_MODEL), F32),
        scratch_shapes=[pltpu.VMEM((2 * len(DIL_PATTERNS), tm, LANES), F32)],
        compiler_params=_params(("parallel",)),
        name="outproj",
    )(*rows, *dil, *consts)


def _rope_tables(T):
    rows = T // GRID_W
    r, c = jnp.meshgrid(jnp.arange(rows), jnp.arange(GRID_W), indexing="ij")
    r = r.reshape(-1).astype(F32)
    c = c.reshape(-1).astype(F32)
    axis_dim = HEAD_DIM // 2
    inv = ROPE_THETA ** (-jnp.arange(0, axis_dim, 2, dtype=F32) / axis_dim)
    ang = jnp.concatenate([r[:, None] * inv, c[:, None] * inv], axis=-1)
    cos, sin = jnp.cos(ang), jnp.sin(ang)
    cos_t = jnp.tile(jnp.concatenate([cos, cos], axis=-1), (1, LANES // HEAD_DIM))
    sin_t = jnp.tile(jnp.concatenate([-sin, sin], axis=-1), (1, LANES // HEAD_DIM))
    return cos_t, sin_t


def _split_w_in(w):
    cuts = np.cumsum((0,) + IN_SPLITS)
    mq, mk, mv, mo, mg, mz, gq, gk, gv, gz, dq, dk, dv, dz = [w[:, cuts[i]:cuts[i + 1]] for i in range(14)]
    pad = jnp.zeros((w.shape[0], GATE_PAD - mg.shape[1]), w.dtype)
    cat = lambda *xs: jnp.concatenate(xs, axis=1).astype(BF16)
    return cat(mq, mk, mv, mg, pad), cat(gq, gk, gv), cat(dq, dk, dv), cat(mo, mz, gz, dz)


def _alibi_slopes():
    return [float(np.exp2(np.float32(-8.0) * np.float32(i) / np.float32(DIL_HEADS)))
            for i in range(1, DIL_HEADS + 1)]


def _layer(x, B, T, p, cos, sin, slopes):
    n = B * T
    m1, g1, *dil_qkv, z = _inproj(x, p["norm_pre"], *p["w_in"], B, T)
    hf, hb = _mlstm(m1.reshape(B, T, M1_W), p["conv_w"], p["conv_b"], p["gate_bias"])
    b = _gqa(g1.reshape(B, T, G1_W), cos, sin, p["q_norm"], p["k_norm"])

    dil_o, dil_l = [], []
    for g, (_, dilation) in enumerate(DIL_PATTERNS):
        o_g, l_g = _dilated(dil_qkv[g], dilation, slopes[g * DIL_HEADS_PER_GROUP:(g + 1) * DIL_HEADS_PER_GROUP])
        dil_o.append(o_g)
        dil_l.append(l_g)

    return _outproj(x, hf.reshape(n, MLSTM_W), hb.reshape(n, MLSTM_W), z, b.reshape(n, GQA_W),
                    dil_o, dil_l, p["out_gain"], p["w_out"], p["norm_post"], T)


def _trunk(x, layers):
    B, T, _ = x.shape
    cos, sin = _rope_tables(T)
    slopes = _alibi_slopes()
    x = x.reshape(B * T, D_MODEL)
    for p in layers:
        x = _layer(x, B, T, p, cos, sin, slopes)
    return x.reshape(B, T, D_MODEL)


def kernel(x_prompt, x_sample, norm_pre, w_in, mlstm_gate_bias, mlstm_conv_w, mlstm_conv_b, mlstm_out_gain,
           gqa_q_norm, gqa_k_norm, w_out, norm_post):
    depth = w_in.shape[0]
    row = lambda a: a.reshape(1, -1).astype(F32)
    layers = []
    for l in range(depth):
        gate_bias = jnp.concatenate(
            [mlstm_gate_bias[l].astype(F32), jnp.zeros((GATE_PAD - 4 * MLSTM_HEADS,), F32)]).reshape(1, GATE_PAD)
        layers.append(dict(
            norm_pre=row(norm_pre[l]),
            w_in=_split_w_in(w_in[l]),
            gate_bias=gate_bias,
            conv_w=mlstm_conv_w[l].astype(F32),
            conv_b=row(mlstm_conv_b[l]),
            out_gain=row(mlstm_out_gain[l]),
            q_norm=row(jnp.tile(gqa_q_norm[l], GQA_Q_HEADS)),
            k_norm=row(jnp.tile(gqa_k_norm[l], GQA_KV_HEADS)),
            w_out=w_out[l].astype(BF16),
            norm_post=row(norm_post[l]),
        ))
    return (_trunk(x_prompt, layers), _trunk(x_sample, layers))
```

```python
import functools
import math

import numpy as np
import jax
import jax.numpy as jnp
from jax import lax
from jax.experimental import pallas as pl
from jax.experimental.pallas import tpu as pltpu

F32 = jnp.float32
BF16 = jnp.bfloat16

D_MODEL = 1024
HEAD_DIM = 64
NORM_EPS = 1e-6
GRID_W = 64
ROPE_THETA = 10000.0
MLSTM_HEADS = 4
MLSTM_W = MLSTM_HEADS * HEAD_DIM
MLSTM_CHUNK = 128
GQA_Q_HEADS = 6
GQA_KV_HEADS = 2
GQA_GROUP = GQA_Q_HEADS // GQA_KV_HEADS
GQA_W = GQA_Q_HEADS * HEAD_DIM
GQA_KV_W = GQA_KV_HEADS * HEAD_DIM
DIL_PATTERNS = ((128, 1), (512, 4), (2048, 16))
DIL_HEADS_PER_GROUP = 2
DIL_HEADS = DIL_HEADS_PER_GROUP * len(DIL_PATTERNS)
DIL_W = DIL_HEADS * HEAD_DIM
DIL_SIDE = 64
IN_SPLITS = (MLSTM_W, MLSTM_W, MLSTM_W, MLSTM_W, 4 * MLSTM_HEADS, MLSTM_W,
             GQA_W, GQA_KV_W, GQA_KV_W, GQA_W, DIL_W, DIL_W, DIL_W, DIL_W)

LANES = 128
SUBLANES = 8
GATE_PAD = LANES
M1_W = 3 * MLSTM_W + GATE_PAD
G1_W = GQA_W + 2 * GQA_KV_W
D1_W = 3 * DIL_W
Z_W = 2 * MLSTM_W + GQA_W + DIL_W
VMEM_LIMIT = 48 * 1024 * 1024
LOG2E = math.log2(math.e)
NEG_INF = float("-inf")


def _sigmoid(x):
    return 1.0 / (1.0 + jnp.exp(-x))


def _silu(x):
    return x * _sigmoid(x)


def _log_sigmoid(x):
    return jnp.minimum(x, 0.0) - jnp.log(1.0 + jnp.exp(-jnp.abs(x)))


def _head_id(shape, axis):
    return lax.broadcasted_iota(jnp.int32, shape, axis) // HEAD_DIM


def _run_interleaved(stage_generators):
    live = list(stage_generators)
    while live:
        live = [g for g in live if next(g, StopIteration) is not StopIteration]


def _params(sem):
    return pltpu.CompilerParams(dimension_semantics=sem, vmem_limit_bytes=VMEM_LIMIT)


def _split_terms(a, terms):
    out = []
    for t in range(terms):
        hi = a.astype(BF16)
        out.append(hi)
        if t + 1 < terms:
            a = a - hi.astype(F32)
    return out


def _inproj_kernel(x_ref, xp_ref, xn_ref, g_ref, wm_ref, wg_ref, wd_ref, wz_ref, cw_ref, cb_ref,
                   oqkv_ref, ogate_ref, og_ref, od1_ref, od4_ref, od16_ref, oz_ref, ds_ref, *, tm, tps):
    def normed(x):
        ms = jnp.mean(x * x, axis=-1, keepdims=True)
        return (x * lax.rsqrt(ms + NORM_EPS) * g_ref[...]).astype(BF16)

    h = normed(x_ref[...])
    ym = jnp.dot(h, wm_ref[...], preferred_element_type=F32)
    qkw = 2 * MLSTM_W
    halo = normed(jnp.concatenate([xp_ref[...], xn_ref[...]], axis=0))
    yh = jnp.dot(halo, wm_ref[:, :qkw], preferred_element_type=F32)
    t = pl.program_id(0) % tps
    prev_row = yh[SUBLANES - 1:SUBLANES, :] * jnp.where(t > 0, 1.0, 0.0).astype(F32)
    next_row = yh[SUBLANES:SUBLANES + 1, :] * jnp.where(t < tps - 1, 1.0, 0.0).astype(F32)
    qk = ym[:, :qkw]
    rid = lax.broadcasted_iota(jnp.int32, (tm, qkw), 0)
    qk_m1 = jnp.where(rid == 0, prev_row, pltpu.roll(qk, 1, 0))
    qk_p1 = jnp.where(rid == tm - 1, next_row, pltpu.roll(qk, tm - 1, 0))
    cw = cw_ref[...]
    act = _silu(qk_m1 * cw[0:1, :] + qk * cw[1:2, :] + qk_p1 * cw[2:3, :] + cb_ref[...])
    oqkv_ref[:, :MLSTM_W] = act[:, :MLSTM_W].astype(BF16)
    oqkv_ref[:, MLSTM_W:qkw] = (act[:, MLSTM_W:] * HEAD_DIM ** -0.5).astype(BF16)
    oqkv_ref[:, qkw:] = ym[:, qkw:3 * MLSTM_W].astype(BF16)
    ogate_ref[...] = ym[:, 3 * MLSTM_W:]
    og_ref[...] = jnp.dot(h, wg_ref[...], preferred_element_type=F32)
    oz_ref[...] = jnp.dot(h, wz_ref[...], preferred_element_type=F32).astype(BF16)
    yd = jnp.dot(h, wd_ref[...], preferred_element_type=F32)
    n_tiles = D1_W // LANES
    for j in range(n_tiles):
        ds_ref[j] = yd[:, j * LANES:(j + 1) * LANES]
    for g, (out_ref, (_, d)) in enumerate(zip((od1_ref, od4_ref, od16_ref), DIL_PATTERNS)):
        for part in range(3):
            src = part * len(DIL_PATTERNS) + g
            for r in range(d):
                rows = ds_ref[src] if d == 1 else ds_ref[src, pl.ds(r, tm // d, stride=d), :]
                out_ref[0, r, :, part * LANES:(part + 1) * LANES] = rows.astype(BF16)


def _inproj(x, g, wm, wg, wd, wz, conv_w, conv_b, B, T, tm=512):
    n = x.shape[0]
    tps = T // tm
    hb = tm // SUBLANES
    row = lambda w: pl.BlockSpec((tm, w), lambda i: (i, 0))
    full = lambda a: pl.BlockSpec(a.shape, lambda i: (0, 0))
    dil_spec = lambda d: pl.BlockSpec((1, d, tm // d, 3 * LANES), lambda i: (i // tps, 0, i % tps, 0))
    dil_shape = lambda d: jax.ShapeDtypeStruct((B, d, T // d, 3 * LANES), BF16)
    dils = [d for _, d in DIL_PATTERNS]
    halo_prev = pl.BlockSpec((SUBLANES, D_MODEL), lambda i: (jnp.maximum(i * hb - 1, 0), 0))
    halo_next = pl.BlockSpec((SUBLANES, D_MODEL), lambda i: (jnp.minimum((i + 1) * hb, n // SUBLANES - 1), 0))
    return pl.pallas_call(
        functools.partial(_inproj_kernel, tm=tm, tps=tps),
        grid=(n // tm,),
        in_specs=[row(D_MODEL), halo_prev, halo_next, full(g), full(wm), full(wg), full(wd), full(wz),
                  full(conv_w), full(conv_b)],
        out_specs=[row(3 * MLSTM_W), row(GATE_PAD), row(G1_W)] + [dil_spec(d) for d in dils] + [row(Z_W)],
        out_shape=[jax.ShapeDtypeStruct((n, 3 * MLSTM_W), BF16), jax.ShapeDtypeStruct((n, GATE_PAD), F32),
                   jax.ShapeDtypeStruct((n, G1_W), F32)]
        + [dil_shape(d) for d in dils] + [jax.ShapeDtypeStruct((n, Z_W), BF16)],
        scratch_shapes=[pltpu.VMEM((D1_W // LANES, tm, LANES), F32)],
        compiler_params=_params(("parallel",)),
        name="inproj",
    )(x, x, x, g, wm, wg, wd, wz, conv_w, conv_b)


MLSTM_EXP2_CAP = 126.0


def _mlstm_consts():
    L, W, H = MLSTM_CHUNK, MLSTM_W, MLSTM_HEADS
    r = np.arange(L)
    tri = np.stack([r[None, :] <= r[:, None], r[None, :] >= r[:, None]]).astype(np.float32)
    lane_head = np.arange(W) // HEAD_DIM
    spread = np.zeros((2, GATE_PAD, W), np.float32)
    neg = np.full((SUBLANES, GATE_PAD), NEG_INF, np.float32)
    for d in range(2):
        fo = (2 * d + 1) * H
        for h in range(H):
            spread[d, fo + h] = lane_head == h
        neg[d, fo:fo + H] = 0.0
    ones_stack = (np.repeat(np.arange(H), L)[:, None] == lane_head[None, :]).astype(np.float32)
    same_head = (lane_head[:, None] == lane_head[None, :]).astype(np.float32)
    return (jnp.asarray(tri, BF16), jnp.asarray(spread, BF16), jnp.asarray(neg), jnp.asarray(ones_stack, BF16),
            jnp.asarray(same_head, BF16), jnp.asarray(same_head))


def _mlstm_direction(qkv_ref, gate_ref, out_ref, c_ref, n_ref, m_ref, gb_ref, tri_ref, spread_ref, neg_ref,
                     os_ref, shb_ref, shf_ref, d, forward):
    L, W, H = MLSTM_CHUNK, MLSTM_W, MLSTM_HEADS
    x = qkv_ref[0]
    q, k, v = x[:, :W], x[:, W:2 * W], x[:, 2 * W:]
    g = gate_ref[0] + gb_ref[...]
    fo = (2 * d + 1) * H
    c_state = c_ref[d]
    n_state = n_ref[d][0:1, :]
    s_heads = [lax.dot_general(q * os_ref[h * L:(h + 1) * L, :], k, (((1,), (1,)), ((), ())),
                               preferred_element_type=F32) for h in range(H)]
    q_c = jnp.dot(q, c_state.astype(BF16), preferred_element_type=F32)
    q_n = jnp.dot((q.astype(F32) * n_state).astype(BF16), shb_ref[...], preferred_element_type=F32)
    cum = sum(jnp.dot(tri_ref[d], t, preferred_element_type=F32)
              for t in _split_terms(_log_sigmoid(g) * LOG2E, 3))
    yield
    u = pltpu.roll(g * LOG2E, H, 1) - cum
    u_t = u.T
    r_i = lax.broadcasted_iota(jnp.int32, (L, L), 0)
    c_i = lax.broadcasted_iota(jnp.int32, (L, L), 1)
    seen = (c_i <= r_i) if forward else (c_i >= r_i)
    gl = lax.broadcasted_iota(jnp.int32, (L, GATE_PAD), 1)
    m_state = m_ref[d][0:1, :]
    m_row = jnp.zeros((L, GATE_PAD), F32)
    w_blocks = []
    for h in range(H):
        col = fo + h
        bc = cum[:, col:col + 1]
        dmat = jnp.where(seen, bc + u_t[col:col + 1, :], NEG_INF)
        m_row_h = jnp.maximum(bc + m_state[:, col:col + 1], jnp.max(dmat, axis=1, keepdims=True))
        w_blocks.append((jnp.exp2(dmat - m_row_h) * s_heads[h]).astype(BF16))
        m_row = jnp.where(gl == col, m_row_h, m_row)

    neg = neg_ref[d:d + 1, :]
    tot_row = L - 1 if forward else 0
    b_tot = cum[tot_row:tot_row + 1, :]
    gv = b_tot + u
    m_new = jnp.maximum(b_tot + m_state, jnp.max(gv, axis=0, keepdims=True))
    decay = jnp.exp2(b_tot + m_state - m_new + neg)
    stack = jnp.concatenate([jnp.exp2(cum + m_state - m_row + neg),
                             jnp.exp2(jnp.minimum(-m_row, MLSTM_EXP2_CAP) + neg),
                             jnp.exp2(gv - m_new + neg),
                             jnp.broadcast_to(decay, (SUBLANES, GATE_PAD))], axis=0)
    spread = sum(jnp.dot(t, spread_ref[d], preferred_element_type=F32) for t in _split_terms(stack, 2))
    w_inter, inv_floor, k_fac, decay_e = spread[:L], spread[L:2 * L], spread[2 * L:3 * L], spread[3 * L:3 * L + 1]
    yield
    w_cat = jnp.concatenate(w_blocks, axis=1)
    v_stack = os_ref[...] * jnp.concatenate([v] * H, axis=0)
    num = w_inter * q_c + jnp.dot(w_cat, v_stack, preferred_element_type=F32)
    den = w_inter * q_n + jnp.dot(w_cat, os_ref[...], preferred_element_type=F32)
    out_ref[0] = num / jnp.maximum(jnp.abs(den), inv_floor)
    yield
    kw = k_fac.astype(BF16) * k
    kv = lax.dot_general(kw, v, (((0,), (0,)), ((), ())), preferred_element_type=F32)
    k_sum = jnp.dot(jnp.ones((2 * SUBLANES, L), BF16), kw, preferred_element_type=F32)[0:1, :]
    c_ref[d] = decay_e * c_state + jnp.where(shf_ref[...] != 0.0, kv, 0.0)
    n_ref[d] = jnp.broadcast_to(decay_e * n_state + k_sum, (SUBLANES, W))
    m_ref[d] = jnp.broadcast_to(jnp.where(neg == 0.0, m_new, 0.0), (SUBLANES, GATE_PAD))


def _mlstm_kernel(qf_ref, gf_ref, qb_ref, gbk_ref, gb_ref, tri_ref, spread_ref, neg_ref, os_ref, shb_ref, shf_ref,
                  hf_ref, hb_ref, c_ref, n_ref, m_ref):
    @pl.when(pl.program_id(1) == 0)
    def _():
        c_ref[...] = jnp.zeros_like(c_ref)
        n_ref[...] = jnp.zeros_like(n_ref)
        m_ref[...] = jnp.zeros_like(m_ref)

    consts = (gb_ref, tri_ref, spread_ref, neg_ref, os_ref, shb_ref, shf_ref)
    _run_interleaved([_mlstm_direction(qf_ref, gf_ref, hf_ref, c_ref, n_ref, m_ref, *consts, 0, True),
                      _mlstm_direction(qb_ref, gbk_ref, hb_ref, c_ref, n_ref, m_ref, *consts, 1, False)])


def _mlstm(qkv, gates, gate_bias):
    B, T, _ = qkv.shape
    L = MLSTM_CHUNK
    nc = T // L
    consts = (gate_bias,) + _mlstm_consts()
    full = lambda a: pl.BlockSpec(a.shape, lambda b, c: (0,) * a.ndim)
    chunk = lambda w, pos: pl.BlockSpec((1, L, w), lambda b, c: (b, pos(c), 0))
    fwd = lambda c: c
    bwd = lambda c: nc - 1 - c
    return pl.pallas_call(
        _mlstm_kernel,
        grid=(B, nc),
        in_specs=[chunk(3 * MLSTM_W, fwd), chunk(GATE_PAD, fwd), chunk(3 * MLSTM_W, bwd), chunk(GATE_PAD, bwd)]
        + [full(a) for a in consts],
        out_specs=[chunk(MLSTM_W, fwd), chunk(MLSTM_W, bwd)],
        out_shape=[jax.ShapeDtypeStruct((B, T, MLSTM_W), F32)] * 2,
        scratch_shapes=[pltpu.VMEM((2, MLSTM_W, MLSTM_W), F32),
                        pltpu.VMEM((2, SUBLANES, MLSTM_W), F32),
                        pltpu.VMEM((2, SUBLANES, GATE_PAD), F32)],
        compiler_params=_params(("parallel", "arbitrary")),
        name="mlstm",
    )(qkv, gates, qkv, gates, *consts)


GQA_SAFE_LOG2 = 56.0
GQA_BOUND_SLACK = 1.0 + 2.0 ** -6
GQA_AUG_W = GQA_Q_HEADS * LANES
GQA_SCALE = HEAD_DIM ** -0.5 * LOG2E


def _rope_tile(a, cos, sin):
    half = HEAD_DIM // 2
    lane = lax.broadcasted_iota(jnp.int32, a.shape, 1) % HEAD_DIM
    rot = jnp.where(lane < half, pltpu.roll(a, LANES - half, 1), pltpu.roll(a, half, 1))
    return a * cos + rot * sin


def _head_sumsq(a):
    w = a.shape[1]
    same_head = jnp.where(_head_id((w, w), 0) == _head_id((w, w), 1), 1.0, 0.0).astype(BF16)
    return sum(jnp.dot(t, same_head, preferred_element_type=F32) for t in _split_terms(a * a, 2))


def _head_rms(a, gain):
    return a * lax.rsqrt(_head_sumsq(a) * (1.0 / HEAD_DIM) + NORM_EPS) * gain


def _gqa_prep_kernel(g_ref, cos_ref, sin_ref, qn_ref, kn_ref, nmb_ref, q_out, kt_out, v_out):
    x = g_ref[0]
    cos = cos_ref[...]
    sin = sin_ref[...]
    q = _head_rms(x[:, :GQA_W], qn_ref[...])
    k = _head_rms(x[:, GQA_W:GQA_W + GQA_KV_W], kn_ref[...])
    v = x[:, GQA_W + GQA_KV_W:]
    lane = lax.broadcasted_iota(jnp.int32, (x.shape[0], LANES), 1)
    first = lane < HEAD_DIM
    bound_col = jnp.where(lane == HEAD_DIM, nmb_ref[...], 0.0)
    ones_col = jnp.where(lane == HEAD_DIM, 1.0, 0.0)
    tiles = []
    for j in range(GQA_W // LANES):
        pair = _rope_tile(q[:, j * LANES:(j + 1) * LANES], cos, sin) * GQA_SCALE
        tiles.append(jnp.where(first, pair, bound_col).astype(BF16))
        tiles.append(jnp.where(first, pltpu.roll(pair, HEAD_DIM, 1), bound_col).astype(BF16))
    q_out[0] = jnp.concatenate(tiles, axis=1)
    k_rot = _rope_tile(k, cos, sin)
    kt_out[0, 0] = jnp.where(first, k_rot, ones_col).T.astype(BF16)
    kt_out[0, 1] = jnp.where(first, pltpu.roll(k_rot, HEAD_DIM, 1), ones_col).T.astype(BF16)
    v_out[0, 0] = jnp.where(first, v, 1.0).astype(BF16)
    v_out[0, 1] = jnp.where(first, pltpu.roll(v, HEAD_DIM, 1), 1.0).astype(BF16)


def _gqa_prep(g1, cos, sin, q_norm, k_norm, neg_mb, tp=512):
    B, T, _ = g1.shape
    full = lambda a: pl.BlockSpec(a.shape, lambda b, i: (0, 0))
    return pl.pallas_call(
        _gqa_prep_kernel,
        grid=(B, T // tp),
        in_specs=[pl.BlockSpec((1, tp, G1_W), lambda b, i: (b, i, 0)),
                  pl.BlockSpec((tp, LANES), lambda b, i: (i, 0)),
                  pl.BlockSpec((tp, LANES), lambda b, i: (i, 0)),
                  full(q_norm), full(k_norm), full(neg_mb)],
        out_specs=[pl.BlockSpec((1, tp, GQA_AUG_W), lambda b, i: (b, i, 0)),
                   pl.BlockSpec((1, GQA_KV_HEADS, LANES, tp), lambda b, i: (b, 0, 0, i)),
                   pl.BlockSpec((1, GQA_KV_HEADS, tp, LANES), lambda b, i: (b, 0, i, 0))],
        out_shape=[jax.ShapeDtypeStruct((B, T, GQA_AUG_W), BF16),
                   jax.ShapeDtypeStruct((B, GQA_KV_HEADS, LANES, T), BF16),
                   jax.ShapeDtypeStruct((B, GQA_KV_HEADS, T, LANES), BF16)],
        compiler_params=_params(("parallel", "parallel")),
        name="gqa_prep",
    )(g1, cos, sin, q_norm, k_norm, neg_mb)


def _gqa_store_heads(o_ref, o, kh, tq):
    for kk in range(GQA_KV_HEADS):
        @pl.when(kh == kk)
        def _():
            for h in range(GQA_GROUP):
                lo = (kk * GQA_GROUP + h) * HEAD_DIM
                o_ref[0, :, lo:lo + HEAD_DIM] = o[h * tq:(h + 1) * tq, :HEAD_DIM].astype(o_ref.dtype)


def _gqa_fast_kernel(q_ref, kt_ref, v_ref, o_ref, acc_ref, *, tq, tk, kc, nk):
    kh = pl.program_id(2)
    ki = pl.program_id(3)

    @pl.when(ki == 0)
    def _():
        acc_ref[...] = jnp.zeros_like(acc_ref)

    qs = jnp.concatenate([q_ref[0, :, h * LANES:(h + 1) * LANES] for h in range(GQA_GROUP)], axis=0)
    acc = None
    for c in range(tk // kc):
        s = jnp.dot(qs, kt_ref[0, 0, :, c * kc:(c + 1) * kc], preferred_element_type=F32)
        p = jnp.exp2(s).astype(BF16)
        d = jnp.dot(p, v_ref[0, 0, c * kc:(c + 1) * kc, :], preferred_element_type=F32)
        acc = d if acc is None else acc + d
    acc_ref[...] += acc

    @pl.when(ki == nk - 1)
    def _():
        a = acc_ref[...]
        _gqa_store_heads(o_ref, a / pltpu.roll(a, HEAD_DIM, 1), kh, tq)


def _gqa_online_kernel(q_ref, kt_ref, v_ref, o_ref, m_ref, acc_ref, *, tq, nk):
    kh = pl.program_id(2)
    ki = pl.program_id(3)

    @pl.when(ki == 0)
    def _():
        m_ref[...] = jnp.full_like(m_ref, NEG_INF)
        acc_ref[...] = jnp.zeros_like(acc_ref)

    qs = jnp.concatenate([q_ref[0, :, h * LANES:h * LANES + HEAD_DIM] for h in range(GQA_GROUP)], axis=0)
    s = jnp.dot(qs, kt_ref[0, 0, :HEAD_DIM, :], preferred_element_type=F32)
    m_prev = m_ref[...]
    m_new = jnp.maximum(m_prev, jnp.max(s, axis=-1, keepdims=True))
    alpha = jnp.exp2(m_prev - m_new)
    p = jnp.exp2(s - m_new[:, :1]).astype(BF16)
    acc_ref[...] = alpha * acc_ref[...] + jnp.dot(p, v_ref[0, 0], preferred_element_type=F32)
    m_ref[...] = m_new

    @pl.when(ki == nk - 1)
    def _():
        a = acc_ref[...]
        _gqa_store_heads(o_ref, a / pltpu.roll(a, HEAD_DIM, 1), kh, tq)


def _gqa_attend(q, kt, v, fast, tq=256):
    B, T, _ = q.shape
    tk = min(4096, T) if fast else 512
    nq, nk = T // tq, T // tk
    m_rows = GQA_GROUP * tq
    acc = pltpu.VMEM((m_rows, LANES), F32)
    if fast:
        body = functools.partial(_gqa_fast_kernel, tq=tq, tk=tk, kc=512, nk=nk)
        scratch = [acc]
    else:
        body = functools.partial(_gqa_online_kernel, tq=tq, nk=nk)
        scratch = [acc, acc]
    return pl.pallas_call(
        body,
        grid=(B, nq, GQA_KV_HEADS, nk),
        in_specs=[pl.BlockSpec((1, tq, GQA_GROUP * LANES), lambda b, i, h, j: (b, i, h)),
                  pl.BlockSpec((1, 1, LANES, tk), lambda b, i, h, j: (b, h, 0, j)),
                  pl.BlockSpec((1, 1, tk, LANES), lambda b, i, h, j: (b, h, j, 0))],
        out_specs=pl.BlockSpec((1, tq, GQA_W), lambda b, i, h, j: (b, i, 0)),
        out_shape=jax.ShapeDtypeStruct((B, T, GQA_W), BF16),
        scratch_shapes=scratch,
        compiler_params=_params(("parallel", "arbitrary", "arbitrary", "arbitrary")),
        name="gqa_fast" if fast else "gqa_online",
    )(q, kt, v)


def _gqa(g1, cos, sin, q_norm, k_norm):
    mb = (HEAD_DIM * GQA_SCALE * GQA_BOUND_SLACK) * jnp.max(jnp.abs(q_norm)) * jnp.max(jnp.abs(k_norm))
    q, kt, v = _gqa_prep(g1, cos, sin, q_norm, k_norm, jnp.full((1, LANES), -mb, F32))
    return lax.cond(mb <= GQA_SAFE_LOG2,
                    functools.partial(_gqa_attend, fast=True),
                    functools.partial(_gqa_attend, fast=False), q, kt, v)


DIL_UNITS = 8


def _dil_kernel(q_ref, k_ref, v_ref, o_ref, lse_ref, *, S, bq, kw, nb, rb, dilation, slopes):
    base = pl.program_id(2) * (nb * bq)
    row = lax.broadcasted_iota(jnp.int32, (bq, kw), 0)
    col = lax.broadcasted_iota(jnp.int32, (bq, kw), 1)
    hid = _head_id((bq, LANES), 1)

    def unit(rr, n):
        p0 = base + n * bq
        if kw == S:
            start = 0
            kwin, vwin = k_ref[0, rr], v_ref[0, rr]
        else:
            start = pl.multiple_of(jnp.clip(p0 - DIL_SIDE, 0, S - kw), DIL_SIDE)
            kwin = k_ref[0, rr, pl.ds(start, kw), :]
            vwin = v_ref[0, rr, pl.ds(start, kw), :]
        q = q_ref[0, rr, n * bq:(n + 1) * bq, :]
        scores = [lax.dot_general(jnp.where(hid == hh, q, jnp.zeros_like(q)), kwin, (((1,), (1,)), ((), ())),
                                  preferred_element_type=F32) for hh in range(DIL_HEADS_PER_GROUP)]
        yield
        rel = jnp.abs(col - row + (start - p0))
        valid = rel <= DIL_SIDE
        dist = (rel * dilation).astype(F32)
        outs, lses = [], []
        for hh in range(DIL_HEADS_PER_GROUP):
            s = jnp.where(valid, scores[hh] * HEAD_DIM ** -0.5 - slopes[hh] * dist, NEG_INF)
            m = jnp.max(s, axis=-1, keepdims=True)
            p = jnp.exp(s - m)
            l = jnp.sum(p, axis=-1, keepdims=True)
            outs.append(jnp.dot(p.astype(BF16), vwin, preferred_element_type=F32) / l)
            lses.append(m + jnp.log(l))
        o_ref[0, rr, n * bq:(n + 1) * bq, :] = jnp.where(hid == 0, outs[0], outs[1]).astype(o_ref.dtype)
        lse_ref[0, rr, n * bq:(n + 1) * bq, :] = jnp.where(hid == 0, lses[0], lses[1])

    _run_interleaved([unit(rr, n) for rr in range(rb) for n in range(nb)])


def _dilated(qkv, dilation, slopes):
    B, d, S, _ = qkv.shape
    bq = min(128, S)
    kw = min(bq + 2 * DIL_SIDE, S)
    nb = min(DIL_UNITS, S // bq)
    rb = min(d, DIL_UNITS // nb)
    seq = lambda part: pl.BlockSpec((1, rb, S, LANES), lambda b, r, i: (b, r, 0, part))
    blk = lambda part: pl.BlockSpec((1, rb, nb * bq, LANES), lambda b, r, i: (b, r, i, part))
    return pl.pallas_call(
        functools.partial(_dil_kernel, S=S, bq=bq, kw=kw, nb=nb, rb=rb, dilation=dilation, slopes=slopes),
        grid=(B, d // rb, S // (nb * bq)),
        in_specs=[blk(0), seq(1), seq(2)],
        out_specs=[blk(0), blk(0)],
        out_shape=[jax.ShapeDtypeStruct((B, d, S, LANES), BF16), jax.ShapeDtypeStruct((B, d, S, LANES), F32)],
        compiler_params=_params(("parallel", "parallel", "arbitrary")),
        name=f"dilated_{dilation}",
    )(qkv, qkv, qkv)


def _outproj_kernel(x_ref, hf_ref, hb_ref, z_ref, b_ref, o0_ref, o1_ref, o2_ref, l0_ref, l1_ref, l2_ref,
                    gain_ref, w_ref, gpost_ref, out_ref, nat_ref, *, tm):
    z = z_ref[...].astype(F32)
    mo = z[:, :MLSTM_W]
    mz = z[:, MLSTM_W:2 * MLSTM_W]
    gz = z[:, 2 * MLSTM_W:2 * MLSTM_W + GQA_W]
    dz = z[:, 2 * MLSTM_W + GQA_W:]

    h = _sigmoid(mo) * (hf_ref[...] + hb_ref[...])
    a = h * lax.rsqrt(_head_sumsq(h) * (1.0 / HEAD_DIM) + NORM_EPS) * gain_ref[...] * _silu(mz)

    b = b_ref[...].astype(F32) * _silu(gz)

    n_groups = len(DIL_PATTERNS)
    for g, (o_ref, l_ref, (_, d)) in enumerate(zip((o0_ref, o1_ref, o2_ref), (l0_ref, l1_ref, l2_ref), DIL_PATTERNS)):
        for r in range(d):
            dst = slice(None) if d == 1 else pl.ds(r, tm // d, stride=d)
            nat_ref[g, dst, :] = o_ref[0, r].astype(F32)
            nat_ref[n_groups + g, dst, :] = l_ref[0, r]
    outs = [nat_ref[g] for g in range(n_groups)]
    lses = [nat_ref[n_groups + g] for g in range(n_groups)]
    mx = jnp.maximum(jnp.maximum(lses[0], lses[1]), lses[2])
    es = [jnp.exp(l - mx) for l in lses]
    inv = 1.0 / (es[0] + es[1] + es[2])
    c = jnp.concatenate([outs[g] * (es[g] * inv) for g in range(3)], axis=1) * _silu(dz)

    w = w_ref[...]
    y = jnp.dot(a.astype(BF16), w[:MLSTM_W], preferred_element_type=F32)
    y += jnp.dot(b.astype(BF16), w[MLSTM_W:MLSTM_W + GQA_W], preferred_element_type=F32)
    y += jnp.dot(c.astype(BF16), w[MLSTM_W + GQA_W:], preferred_element_type=F32)
    ms = jnp.mean(y * y, axis=-1, keepdims=True)
    out_ref[...] = x_ref[...] + y * lax.rsqrt(ms + NORM_EPS) * gpost_ref[...]


def _outproj(x, hf, hb, z, b, dil_o, dil_l, gain, w, gpost, T, tm=256):
    n = x.shape[0]
    tps = T // tm
    row = lambda a: pl.BlockSpec((tm, a.shape[1]), lambda i: (i, 0))
    full = lambda a: pl.BlockSpec(a.shape, lambda i: (0, 0))
    cls = lambda a: pl.BlockSpec((1, a.shape[1], tm // a.shape[1], LANES), lambda i: (i // tps, 0, i % tps, 0))
    rows = [x, hf, hb, z, b]
    dil = list(dil_o) + list(dil_l)
    consts = [gain, w, gpost]
    return pl.pallas_call(
        functools.partial(_outproj_kernel, tm=tm),
        grid=(n // tm,),
        in_specs=[row(a) for a in rows] + [cls(a) for a in dil] + [full(a) for a in consts],
        out_specs=pl.BlockSpec((tm, D_MODEL), lambda i: (i, 0)),
        out_shape=jax.ShapeDtypeStruct((n, D_MODEL), F32),
        scratch_shapes=[pltpu.VMEM((2 * len(DIL_PATTERNS), tm, LANES), F32)],
        compiler_params=_params(("parallel",)),
        name="outproj",
    )(*rows, *dil, *consts)


def _rope_tables(T):
    rows = T // GRID_W
    r, c = jnp.meshgrid(jnp.arange(rows), jnp.arange(GRID_W), indexing="ij")
    r = r.reshape(-1).astype(F32)
    c = c.reshape(-1).astype(F32)
    axis_dim = HEAD_DIM // 2
    inv = ROPE_THETA ** (-jnp.arange(0, axis_dim, 2, dtype=F32) / axis_dim)
    ang = jnp.concatenate([r[:, None] * inv, c[:, None] * inv], axis=-1)
    cos, sin = jnp.cos(ang), jnp.sin(ang)
    cos_t = jnp.tile(jnp.concatenate([cos, cos], axis=-1), (1, LANES // HEAD_DIM))
    sin_t = jnp.tile(jnp.concatenate([-sin, sin], axis=-1), (1, LANES // HEAD_DIM))
    return cos_t, sin_t


def _split_w_in(w):
    cuts = np.cumsum((0,) + IN_SPLITS)
    mq, mk, mv, mo, mg, mz, gq, gk, gv, gz, dq, dk, dv, dz = [w[:, cuts[i]:cuts[i + 1]] for i in range(14)]
    pad = jnp.zeros((w.shape[0], GATE_PAD - mg.shape[1]), w.dtype)
    cat = lambda *xs: jnp.concatenate(xs, axis=1).astype(BF16)
    return cat(mq, mk, mv, mg, pad), cat(gq, gk, gv), cat(dq, dk, dv), cat(mo, mz, gz, dz)


def _alibi_slopes():
    return [float(np.exp2(np.float32(-8.0) * np.float32(i) / np.float32(DIL_HEADS)))
            for i in range(1, DIL_HEADS + 1)]


def _layer(x, B, T, p, cos, sin, slopes):
    n = B * T
    m_qkv, m_gates, g1, *dil_qkv, z = _inproj(x, p["norm_pre"], *p["w_in"], p["conv_w"], p["conv_b"], B, T)
    hf, hb = _mlstm(m_qkv.reshape(B, T, 3 * MLSTM_W), m_gates.reshape(B, T, GATE_PAD), p["gate_bias"])
    b = _gqa(g1.reshape(B, T, G1_W), cos, sin, p["q_norm"], p["k_norm"])

    dil_o, dil_l = [], []
    for g, (_, dilation) in enumerate(DIL_PATTERNS):
        o_g, l_g = _dilated(dil_qkv[g], dilation, slopes[g * DIL_HEADS_PER_GROUP:(g + 1) * DIL_HEADS_PER_GROUP])
        dil_o.append(o_g)
        dil_l.append(l_g)

    return _outproj(x, hf.reshape(n, MLSTM_W), hb.reshape(n, MLSTM_W), z, b.reshape(n, GQA_W),
                    dil_o, dil_l, p["out_gain"], p["w_out"], p["norm_post"], T)


def _trunk(x, layers):
    B, T, _ = x.shape
    cos, sin = _rope_tables(T)
    slopes = _alibi_slopes()
    x = x.reshape(B * T, D_MODEL)
    for p in layers:
        x = _layer(x, B, T, p, cos, sin, slopes)
    return x.reshape(B, T, D_MODEL)


def kernel(x_prompt, x_sample, norm_pre, w_in, mlstm_gate_bias, mlstm_conv_w, mlstm_conv_b, mlstm_out_gain,
           gqa_q_norm, gqa_k_norm, w_out, norm_post):
    depth = w_in.shape[0]
    row = lambda a: a.reshape(1, -1).astype(F32)
    layers = []
    for l in range(depth):
        gate_bias = jnp.concatenate(
            [mlstm_gate_bias[l].astype(F32), jnp.zeros((GATE_PAD - 4 * MLSTM_HEADS,), F32)]).reshape(1, GATE_PAD)
        layers.append(dict(
            norm_pre=row(norm_pre[l]),
            w_in=_split_w_in(w_in[l]),
            gate_bias=gate_bias,
            conv_w=mlstm_conv_w[l].astype(F32),
            conv_b=row(mlstm_conv_b[l]),
            out_gain=row(mlstm_out_gain[l]),
            q_norm=row(jnp.tile(gqa_q_norm[l], GQA_Q_HEADS)),
            k_norm=row(jnp.tile(gqa_k_norm[l], GQA_KV_HEADS)),
            w_out=w_out[l].astype(BF16),
            norm_post=row(norm_post[l]),
        ))
    return (_trunk(x_prompt, layers), _trunk(x_sample, layers))
```

```python
import functools
import math

import numpy as np
import jax
import jax.numpy as jnp
from jax import lax
from jax.experimental import pallas as pl
from jax.experimental.pallas import tpu as pltpu

F32 = jnp.float32
BF16 = jnp.bfloat16

D_MODEL = 1024
HEAD_DIM = 64
NORM_EPS = 1e-6
GRID_W = 64
ROPE_THETA = 10000.0
MLSTM_HEADS = 4
MLSTM_W = MLSTM_HEADS * HEAD_DIM
MLSTM_CHUNK = 128
GQA_Q_HEADS = 6
GQA_KV_HEADS = 2
GQA_GROUP = GQA_Q_HEADS // GQA_KV_HEADS
GQA_W = GQA_Q_HEADS * HEAD_DIM
GQA_KV_W = GQA_KV_HEADS * HEAD_DIM
DIL_PATTERNS = ((128, 1), (512, 4), (2048, 16))
DIL_HEADS_PER_GROUP = 2
DIL_HEADS = DIL_HEADS_PER_GROUP * len(DIL_PATTERNS)
DIL_W = DIL_HEADS * HEAD_DIM
DIL_SIDE = 64
IN_SPLITS = (MLSTM_W, MLSTM_W, MLSTM_W, MLSTM_W, 4 * MLSTM_HEADS, MLSTM_W,
             GQA_W, GQA_KV_W, GQA_KV_W, GQA_W, DIL_W, DIL_W, DIL_W, DIL_W)

LANES = 128
SUBLANES = 8
GATE_PAD = LANES
M1_W = 3 * MLSTM_W + GATE_PAD
G1_W = GQA_W + 2 * GQA_KV_W
D1_W = 3 * DIL_W
Z_W = 2 * MLSTM_W + GQA_W + DIL_W
VMEM_LIMIT = 48 * 1024 * 1024
LOG2E = math.log2(math.e)
NEG_INF = float("-inf")


def _sigmoid(x):
    return 1.0 / (1.0 + jnp.exp(-x))


def _silu(x):
    return x * _sigmoid(x)


def _log_sigmoid(x):
    return jnp.minimum(x, 0.0) - jnp.log(1.0 + jnp.exp(-jnp.abs(x)))


def _head_id(shape, axis):
    return lax.broadcasted_iota(jnp.int32, shape, axis) // HEAD_DIM


def _run_interleaved(stage_generators):
    live = list(stage_generators)
    while live:
        live = [g for g in live if next(g, StopIteration) is not StopIteration]


def _params(sem):
    return pltpu.CompilerParams(dimension_semantics=sem, vmem_limit_bytes=VMEM_LIMIT)


def _split_terms(a, terms):
    out = []
    for t in range(terms):
        hi = a.astype(BF16)
        out.append(hi)
        if t + 1 < terms:
            a = a - hi.astype(F32)
    return out


def _inproj_kernel(x_ref, xp_ref, xn_ref, g_ref, wm_ref, wg_ref, wd_ref, wz_ref, cw_ref, cb_ref,
                   oqkv_ref, ogate_ref, og_ref, od1_ref, od4_ref, od16_ref, oz_ref, ds_ref, *, tm, tps):
    def normed(x):
        ms = jnp.mean(x * x, axis=-1, keepdims=True)
        return (x * lax.rsqrt(ms + NORM_EPS) * g_ref[...]).astype(BF16)

    h = normed(x_ref[...])
    ym = jnp.dot(h, wm_ref[...], preferred_element_type=F32)
    qkw = 2 * MLSTM_W
    halo = normed(jnp.concatenate([xp_ref[...], xn_ref[...]], axis=0))
    yh = jnp.dot(halo, wm_ref[:, :qkw], preferred_element_type=F32)
    t = pl.program_id(0) % tps
    prev_row = yh[SUBLANES - 1:SUBLANES, :] * jnp.where(t > 0, 1.0, 0.0).astype(F32)
    next_row = yh[SUBLANES:SUBLANES + 1, :] * jnp.where(t < tps - 1, 1.0, 0.0).astype(F32)
    qk = ym[:, :qkw]
    rid = lax.broadcasted_iota(jnp.int32, (tm, qkw), 0)
    qk_m1 = jnp.where(rid == 0, prev_row, pltpu.roll(qk, 1, 0))
    qk_p1 = jnp.where(rid == tm - 1, next_row, pltpu.roll(qk, tm - 1, 0))
    cw = cw_ref[...]
    act = _silu(qk_m1 * cw[0:1, :] + qk * cw[1:2, :] + qk_p1 * cw[2:3, :] + cb_ref[...])
    oqkv_ref[:, :MLSTM_W] = act[:, :MLSTM_W].astype(BF16)
    oqkv_ref[:, MLSTM_W:qkw] = (act[:, MLSTM_W:] * HEAD_DIM ** -0.5).astype(BF16)
    oqkv_ref[:, qkw:] = ym[:, qkw:3 * MLSTM_W].astype(BF16)
    ogate_ref[...] = ym[:, 3 * MLSTM_W:]
    og_ref[...] = jnp.dot(h, wg_ref[...], preferred_element_type=F32)
    oz_ref[...] = jnp.dot(h, wz_ref[...], preferred_element_type=F32).astype(BF16)
    yd = jnp.dot(h, wd_ref[...], preferred_element_type=F32)
    n_tiles = D1_W // LANES
    for j in range(n_tiles):
        ds_ref[j] = yd[:, j * LANES:(j + 1) * LANES]
    for g, (out_ref, (_, d)) in enumerate(zip((od1_ref, od4_ref, od16_ref), DIL_PATTERNS)):
        for part in range(3):
            src = part * len(DIL_PATTERNS) + g
            for r in range(d):
                rows = ds_ref[src] if d == 1 else ds_ref[src, pl.ds(r, tm // d, stride=d), :]
                out_ref[0, r, :, part * LANES:(part + 1) * LANES] = rows.astype(BF16)


def _inproj(x, g, wm, wg, wd, wz, conv_w, conv_b, B, T, tm=512):
    n = x.shape[0]
    tps = T // tm
    hb = tm // SUBLANES
    row = lambda w: pl.BlockSpec((tm, w), lambda i: (i, 0))
    full = lambda a: pl.BlockSpec(a.shape, lambda i: (0, 0))
    dil_spec = lambda d: pl.BlockSpec((1, d, tm // d, 3 * LANES), lambda i: (i // tps, 0, i % tps, 0))
    dil_shape = lambda d: jax.ShapeDtypeStruct((B, d, T // d, 3 * LANES), BF16)
    dils = [d for _, d in DIL_PATTERNS]
    halo_prev = pl.BlockSpec((SUBLANES, D_MODEL), lambda i: (jnp.maximum(i * hb - 1, 0), 0))
    halo_next = pl.BlockSpec((SUBLANES, D_MODEL), lambda i: (jnp.minimum((i + 1) * hb, n // SUBLANES - 1), 0))
    return pl.pallas_call(
        functools.partial(_inproj_kernel, tm=tm, tps=tps),
        grid=(n // tm,),
        in_specs=[row(D_MODEL), halo_prev, halo_next, full(g), full(wm), full(wg), full(wd), full(wz),
                  full(conv_w), full(conv_b)],
        out_specs=[row(3 * MLSTM_W), row(GATE_PAD), row(G1_W)] + [dil_spec(d) for d in dils] + [row(Z_W)],
        out_shape=[jax.ShapeDtypeStruct((n, 3 * MLSTM_W), BF16), jax.ShapeDtypeStruct((n, GATE_PAD), F32),
                   jax.ShapeDtypeStruct((n, G1_W), F32)]
        + [dil_shape(d) for d in dils] + [jax.ShapeDtypeStruct((n, Z_W), BF16)],
        scratch_shapes=[pltpu.VMEM((D1_W // LANES, tm, LANES), F32)],
        compiler_params=_params(("parallel",)),
        name="inproj",
    )(x, x, x, g, wm, wg, wd, wz, conv_w, conv_b)


MLSTM_EXP2_CAP = 126.0


def _mlstm_consts():
    L, W, H = MLSTM_CHUNK, MLSTM_W, MLSTM_HEADS
    r = np.arange(L)
    tri = np.stack([r[None, :] <= r[:, None], r[None, :] >= r[:, None]]).astype(np.float32)
    lane_head = np.arange(W) // HEAD_DIM
    spread = np.zeros((2, GATE_PAD, W), np.float32)
    neg = np.full((SUBLANES, GATE_PAD), NEG_INF, np.float32)
    for d in range(2):
        fo = (2 * d + 1) * H
        for h in range(H):
            spread[d, fo + h] = lane_head == h
        neg[d, fo:fo + H] = 0.0
    ones_stack = (np.repeat(np.arange(H), L)[:, None] == lane_head[None, :]).astype(np.float32)
    same_head = (lane_head[:, None] == lane_head[None, :]).astype(np.float32)
    return (jnp.asarray(tri, BF16), jnp.asarray(spread, BF16), jnp.asarray(neg), jnp.asarray(ones_stack, BF16),
            jnp.asarray(same_head, BF16), jnp.asarray(same_head))


def _mlstm_direction(qkv_ref, gate_ref, out_ref, c_ref, n_ref, m_ref, gb_ref, tri_ref, spread_ref, neg_ref,
                     os_ref, shb_ref, shf_ref, d, forward):
    L, W, H = MLSTM_CHUNK, MLSTM_W, MLSTM_HEADS
    x = qkv_ref[0]
    q, k, v = x[:, :W], x[:, W:2 * W], x[:, 2 * W:]
    g = gate_ref[0] + gb_ref[...]
    fo = (2 * d + 1) * H
    c_state = c_ref[d]
    n_state = n_ref[d][0:1, :]
    s_heads = [lax.dot_general(q * os_ref[h * L:(h + 1) * L, :], k, (((1,), (1,)), ((), ())),
                               preferred_element_type=F32) for h in range(H)]
    q_c = jnp.dot(q, c_state.astype(BF16), preferred_element_type=F32)
    q_n = jnp.dot((q.astype(F32) * n_state).astype(BF16), shb_ref[...], preferred_element_type=F32)
    cum = sum(jnp.dot(tri_ref[d], t, preferred_element_type=F32)
              for t in _split_terms(_log_sigmoid(g) * LOG2E, 3))
    yield
    u = pltpu.roll(g * LOG2E, H, 1) - cum
    u_t = u.T
    r_i = lax.broadcasted_iota(jnp.int32, (L, L), 0)
    c_i = lax.broadcasted_iota(jnp.int32, (L, L), 1)
    seen = (c_i <= r_i) if forward else (c_i >= r_i)
    gl = lax.broadcasted_iota(jnp.int32, (L, GATE_PAD), 1)
    m_state = m_ref[d][0:1, :]
    m_row = jnp.zeros((L, GATE_PAD), F32)
    w_blocks = []
    for h in range(H):
        col = fo + h
        bc = cum[:, col:col + 1]
        dmat = jnp.where(seen, bc + u_t[col:col + 1, :], NEG_INF)
        m_row_h = jnp.maximum(bc + m_state[:, col:col + 1], jnp.max(dmat, axis=1, keepdims=True))
        w_blocks.append((jnp.exp2(dmat - m_row_h) * s_heads[h]).astype(BF16))
        m_row = jnp.where(gl == col, m_row_h, m_row)

    neg = neg_ref[d:d + 1, :]
    tot_row = L - 1 if forward else 0
    b_tot = cum[tot_row:tot_row + 1, :]
    gv = b_tot + u
    m_new = jnp.maximum(b_tot + m_state, jnp.max(gv, axis=0, keepdims=True))
    decay = jnp.exp2(b_tot + m_state - m_new + neg)
    stack = jnp.concatenate([jnp.exp2(cum + m_state - m_row + neg),
                             jnp.exp2(jnp.minimum(-m_row, MLSTM_EXP2_CAP) + neg),
                             jnp.exp2(gv - m_new + neg),
                             jnp.broadcast_to(decay, (SUBLANES, GATE_PAD))], axis=0)
    spread = sum(jnp.dot(t, spread_ref[d], preferred_element_type=F32) for t in _split_terms(stack, 2))
    w_inter, inv_floor, k_fac, decay_e = spread[:L], spread[L:2 * L], spread[2 * L:3 * L], spread[3 * L:3 * L + 1]
    yield
    w_cat = jnp.concatenate(w_blocks, axis=1)
    v_stack = os_ref[...] * jnp.concatenate([v] * H, axis=0)
    num = w_inter * q_c + jnp.dot(w_cat, v_stack, preferred_element_type=F32)
    den = w_inter * q_n + jnp.dot(w_cat, os_ref[...], preferred_element_type=F32)
    out_ref[0] = num / jnp.maximum(jnp.abs(den), inv_floor)
    yield
    kw = k_fac.astype(BF16) * k
    kv = lax.dot_general(kw, v, (((0,), (0,)), ((), ())), preferred_element_type=F32)
    k_sum = jnp.dot(jnp.ones((2 * SUBLANES, L), BF16), kw, preferred_element_type=F32)[0:1, :]
    c_ref[d] = decay_e * c_state + jnp.where(shf_ref[...] != 0.0, kv, 0.0)
    n_ref[d] = jnp.broadcast_to(decay_e * n_state + k_sum, (SUBLANES, W))
    m_ref[d] = jnp.broadcast_to(jnp.where(neg == 0.0, m_new, 0.0), (SUBLANES, GATE_PAD))


def _mlstm_kernel(qf_ref, gf_ref, qb_ref, gbk_ref, gb_ref, tri_ref, spread_ref, neg_ref, os_ref, shb_ref, shf_ref,
                  hf_ref, hb_ref, c_ref, n_ref, m_ref):
    @pl.when(pl.program_id(1) == 0)
    def _():
        c_ref[...] = jnp.zeros_like(c_ref)
        n_ref[...] = jnp.zeros_like(n_ref)
        m_ref[...] = jnp.zeros_like(m_ref)

    consts = (gb_ref, tri_ref, spread_ref, neg_ref, os_ref, shb_ref, shf_ref)
    _run_interleaved([_mlstm_direction(qf_ref, gf_ref, hf_ref, c_ref, n_ref, m_ref, *consts, 0, True),
                      _mlstm_direction(qb_ref, gbk_ref, hb_ref, c_ref, n_ref, m_ref, *consts, 1, False)])


def _mlstm(qkv, gates, gate_bias):
    B, T, _ = qkv.shape
    L = MLSTM_CHUNK
    nc = T // L
    consts = (gate_bias,) + _mlstm_consts()
    full = lambda a: pl.BlockSpec(a.shape, lambda b, c: (0,) * a.ndim)
    chunk = lambda w, pos: pl.BlockSpec((1, L, w), lambda b, c: (b, pos(c), 0))
    fwd = lambda c: c
    bwd = lambda c: nc - 1 - c
    return pl.pallas_call(
        _mlstm_kernel,
        grid=(B, nc),
        in_specs=[chunk(3 * MLSTM_W, fwd), chunk(GATE_PAD, fwd), chunk(3 * MLSTM_W, bwd), chunk(GATE_PAD, bwd)]
        + [full(a) for a in consts],
        out_specs=[chunk(MLSTM_W, fwd), chunk(MLSTM_W, bwd)],
        out_shape=[jax.ShapeDtypeStruct((B, T, MLSTM_W), F32)] * 2,
        scratch_shapes=[pltpu.VMEM((2, MLSTM_W, MLSTM_W), F32),
                        pltpu.VMEM((2, SUBLANES, MLSTM_W), F32),
                        pltpu.VMEM((2, SUBLANES, GATE_PAD), F32)],
        compiler_params=_params(("parallel", "arbitrary")),
        name="mlstm",
    )(qkv, gates, qkv, gates, *consts)


GQA_SAFE_LOG2 = 56.0
GQA_BOUND_SLACK = 1.0 + 2.0 ** -6
GQA_SCALE = HEAD_DIM ** -0.5 * LOG2E


def _rope_tile(a, cos, sin):
    half = HEAD_DIM // 2
    lane = lax.broadcasted_iota(jnp.int32, a.shape, 1) % HEAD_DIM
    rot = jnp.where(lane < half, pltpu.roll(a, LANES - half, 1), pltpu.roll(a, half, 1))
    return a * cos + rot * sin


def _head_sumsq(a):
    w = a.shape[1]
    same_head = jnp.where(_head_id((w, w), 0) == _head_id((w, w), 1), 1.0, 0.0).astype(BF16)
    return sum(jnp.dot(t, same_head, preferred_element_type=F32) for t in _split_terms(a * a, 2))


def _head_rms(a, gain):
    return a * lax.rsqrt(_head_sumsq(a) * (1.0 / HEAD_DIM) + NORM_EPS) * gain


def _gqa_prep_kernel(g_ref, cos_ref, sin_ref, qn_ref, kn_ref, nmb_ref, qt_out, k_out, vt_out):
    x = g_ref[0]
    cos = cos_ref[...]
    sin = sin_ref[...]
    q = _head_rms(x[:, :GQA_W], qn_ref[...])
    k = _head_rms(x[:, GQA_W:GQA_W + GQA_KV_W], kn_ref[...])
    v = x[:, GQA_W + GQA_KV_W:]
    lane = lax.broadcasted_iota(jnp.int32, (x.shape[0], LANES), 1)
    first = lane < HEAD_DIM
    bound_col = jnp.where(lane == HEAD_DIM, nmb_ref[...], 0.0)
    ones_col = jnp.where(lane == HEAD_DIM, 1.0, 0.0)
    q_t = []
    for j in range(GQA_W // LANES):
        pair = _rope_tile(q[:, j * LANES:(j + 1) * LANES], cos, sin) * GQA_SCALE
        q_t.append(jnp.where(first, pair, bound_col).T.astype(BF16))
        q_t.append(jnp.where(first, pltpu.roll(pair, HEAD_DIM, 1), bound_col).T.astype(BF16))
    for kk in range(GQA_KV_HEADS):
        qt_out[0, kk, 0] = jnp.concatenate(q_t[kk * GQA_GROUP:(kk + 1) * GQA_GROUP], axis=1)
    k_rot = _rope_tile(k, cos, sin)
    k_out[0, 0] = jnp.where(first, k_rot, ones_col).astype(BF16)
    k_out[0, 1] = jnp.where(first, pltpu.roll(k_rot, HEAD_DIM, 1), ones_col).astype(BF16)
    vt_out[0, 0] = jnp.where(first, v, 1.0).T.astype(BF16)
    vt_out[0, 1] = jnp.where(first, pltpu.roll(v, HEAD_DIM, 1), 1.0).T.astype(BF16)


def _gqa_prep(g1, cos, sin, q_norm, k_norm, neg_mb, tq):
    B, T, _ = g1.shape
    nq = T // tq
    full = lambda a: pl.BlockSpec(a.shape, lambda b, i: (0, 0))
    return pl.pallas_call(
        _gqa_prep_kernel,
        grid=(B, nq),
        in_specs=[pl.BlockSpec((1, tq, G1_W), lambda b, i: (b, i, 0)),
                  pl.BlockSpec((tq, LANES), lambda b, i: (i, 0)),
                  pl.BlockSpec((tq, LANES), lambda b, i: (i, 0)),
                  full(q_norm), full(k_norm), full(neg_mb)],
        out_specs=[pl.BlockSpec((1, GQA_KV_HEADS, 1, LANES, GQA_GROUP * tq), lambda b, i: (b, 0, i, 0, 0)),
                   pl.BlockSpec((1, GQA_KV_HEADS, tq, LANES), lambda b, i: (b, 0, i, 0)),
                   pl.BlockSpec((1, GQA_KV_HEADS, LANES, tq), lambda b, i: (b, 0, 0, i))],
        out_shape=[jax.ShapeDtypeStruct((B, GQA_KV_HEADS, nq, LANES, GQA_GROUP * tq), BF16),
                   jax.ShapeDtypeStruct((B, GQA_KV_HEADS, T, LANES), BF16),
                   jax.ShapeDtypeStruct((B, GQA_KV_HEADS, LANES, T), BF16)],
        compiler_params=_params(("parallel", "parallel")),
        name="gqa_prep",
    )(g1, cos, sin, q_norm, k_norm, neg_mb)


def _gqa_finish(o_ref, acc_t, kh, tq):
    den = acc_t[HEAD_DIM:, :]
    o = (acc_t / jnp.concatenate([den, den], axis=0)).T
    for kk in range(GQA_KV_HEADS):
        @pl.when(kh == kk)
        def _():
            for h in range(GQA_GROUP):
                lo = (kk * GQA_GROUP + h) * HEAD_DIM
                o_ref[0, :, lo:lo + HEAD_DIM] = o[h * tq:(h + 1) * tq, :HEAD_DIM].astype(o_ref.dtype)


def _gqa_fast_kernel(qt_ref, k_ref, vt_ref, o_ref, acc_ref, *, tq, tk, kc, nk):
    kh = pl.program_id(2)
    ki = pl.program_id(3)

    @pl.when(ki == 0)
    def _():
        acc_ref[...] = jnp.zeros_like(acc_ref)

    qt = qt_ref[0, 0, 0]
    scores = lambda c: jnp.dot(k_ref[0, 0, c * kc:(c + 1) * kc, :], qt, preferred_element_type=F32)
    n_chunks = tk // kc
    acc = None
    st_next = scores(0)
    for c in range(n_chunks):
        st = st_next
        if c + 1 < n_chunks:
            st_next = scores(c + 1)
        pt = jnp.exp2(st).astype(BF16)
        d = jnp.dot(vt_ref[0, 0, :, c * kc:(c + 1) * kc], pt, preferred_element_type=F32)
        acc = d if acc is None else acc + d
    acc_ref[...] += acc

    @pl.when(ki == nk - 1)
    def _():
        _gqa_finish(o_ref, acc_ref[...], kh, tq)


def _gqa_online_kernel(qt_ref, k_ref, vt_ref, o_ref, m_ref, acc_ref, *, tq, nk):
    kh = pl.program_id(2)
    ki = pl.program_id(3)

    @pl.when(ki == 0)
    def _():
        m_ref[...] = jnp.full_like(m_ref, NEG_INF)
        acc_ref[...] = jnp.zeros_like(acc_ref)

    st = jnp.dot(k_ref[0, 0][:, :HEAD_DIM], qt_ref[0, 0, 0][:HEAD_DIM, :],
                 preferred_element_type=F32)
    m_prev = m_ref[0:1, :]
    m_new = jnp.maximum(m_prev, jnp.max(st, axis=0, keepdims=True))
    alpha = jnp.exp2(m_prev - m_new)
    pt = jnp.exp2(st - m_new).astype(BF16)
    acc_ref[...] = alpha * acc_ref[...] + jnp.dot(vt_ref[0, 0], pt, preferred_element_type=F32)
    m_ref[...] = jnp.broadcast_to(m_new, m_ref.shape)

    @pl.when(ki == nk - 1)
    def _():
        _gqa_finish(o_ref, acc_ref[...], kh, tq)


def _gqa_attend(qt, k, vt, fast, tq):
    B, _, T, _ = k.shape
    tk = min(4096, T) if fast else 512
    nq, nk = T // tq, T // tk
    n_cols = GQA_GROUP * tq
    acc = pltpu.VMEM((LANES, n_cols), F32)
    if fast:
        body = functools.partial(_gqa_fast_kernel, tq=tq, tk=tk, kc=256, nk=nk)
        scratch = [acc]
    else:
        body = functools.partial(_gqa_online_kernel, tq=tq, nk=nk)
        scratch = [pltpu.VMEM((SUBLANES, n_cols), F32), acc]
    return pl.pallas_call(
        body,
        grid=(B, nq, GQA_KV_HEADS, nk),
        in_specs=[pl.BlockSpec((1, 1, 1, LANES, n_cols), lambda b, i, h, j: (b, h, i, 0, 0)),
                  pl.BlockSpec((1, 1, tk, LANES), lambda b, i, h, j: (b, h, j, 0)),
                  pl.BlockSpec((1, 1, LANES, tk), lambda b, i, h, j: (b, h, 0, j))],
        out_specs=pl.BlockSpec((1, tq, GQA_W), lambda b, i, h, j: (b, i, 0)),
        out_shape=jax.ShapeDtypeStruct((B, T, GQA_W), BF16),
        scratch_shapes=scratch,
        compiler_params=_params(("parallel", "arbitrary", "arbitrary", "arbitrary")),
        name="gqa_fast" if fast else "gqa_online",
    )(qt, k, vt)


def _gqa(g1, cos, sin, q_norm, k_norm, tq=512):
    mb = (HEAD_DIM * GQA_SCALE * GQA_BOUND_SLACK) * jnp.max(jnp.abs(q_norm)) * jnp.max(jnp.abs(k_norm))
    qt, k, vt = _gqa_prep(g1, cos, sin, q_norm, k_norm, jnp.full((1, LANES), -mb, F32), tq)
    return lax.cond(mb <= GQA_SAFE_LOG2,
                    functools.partial(_gqa_attend, fast=True, tq=tq),
                    functools.partial(_gqa_attend, fast=False, tq=tq), qt, k, vt)


DIL_UNITS = 8


def _dil_kernel(q_ref, k_ref, v_ref, o_ref, lse_ref, *, S, bq, kw, nb, rb, dilation, slopes):
    base = pl.program_id(2) * (nb * bq)
    row = lax.broadcasted_iota(jnp.int32, (bq, kw), 0)
    col = lax.broadcasted_iota(jnp.int32, (bq, kw), 1)
    hid = _head_id((bq, LANES), 1)

    def unit(rr, n):
        p0 = base + n * bq
        if kw == S:
            start = 0
            kwin, vwin = k_ref[0, rr], v_ref[0, rr]
        else:
            start = pl.multiple_of(jnp.clip(p0 - DIL_SIDE, 0, S - kw), DIL_SIDE)
            kwin = k_ref[0, rr, pl.ds(start, kw), :]
            vwin = v_ref[0, rr, pl.ds(start, kw), :]
        q = q_ref[0, rr, n * bq:(n + 1) * bq, :]
        scores = [lax.dot_general(jnp.where(hid == hh, q, jnp.zeros_like(q)), kwin, (((1,), (1,)), ((), ())),
                                  preferred_element_type=F32) for hh in range(DIL_HEADS_PER_GROUP)]
        yield
        rel = jnp.abs(col - row + (start - p0))
        valid = rel <= DIL_SIDE
        dist = (rel * dilation).astype(F32)
        outs, lses = [], []
        for hh in range(DIL_HEADS_PER_GROUP):
            s = jnp.where(valid, scores[hh] * HEAD_DIM ** -0.5 - slopes[hh] * dist, NEG_INF)
            m = jnp.max(s, axis=-1, keepdims=True)
            p = jnp.exp(s - m)
            l = jnp.sum(p, axis=-1, keepdims=True)
            outs.append(jnp.dot(p.astype(BF16), vwin, preferred_element_type=F32) / l)
            lses.append(m + jnp.log(l))
        o_ref[0, rr, n * bq:(n + 1) * bq, :] = jnp.where(hid == 0, outs[0], outs[1]).astype(o_ref.dtype)
        lse_ref[0, rr, n * bq:(n + 1) * bq, :] = jnp.where(hid == 0, lses[0], lses[1])

    _run_interleaved([unit(rr, n) for rr in range(rb) for n in range(nb)])


def _dilated(qkv, dilation, slopes):
    B, d, S, _ = qkv.shape
    bq = min(128, S)
    kw = min(bq + 2 * DIL_SIDE, S)
    nb = min(DIL_UNITS, S // bq)
    rb = min(d, DIL_UNITS // nb)
    seq = lambda part: pl.BlockSpec((1, rb, S, LANES), lambda b, r, i: (b, r, 0, part))
    blk = lambda part: pl.BlockSpec((1, rb, nb * bq, LANES), lambda b, r, i: (b, r, i, part))
    return pl.pallas_call(
        functools.partial(_dil_kernel, S=S, bq=bq, kw=kw, nb=nb, rb=rb, dilation=dilation, slopes=slopes),
        grid=(B, d // rb, S // (nb * bq)),
        in_specs=[blk(0), seq(1), seq(2)],
        out_specs=[blk(0), blk(0)],
        out_shape=[jax.ShapeDtypeStruct((B, d, S, LANES), BF16), jax.ShapeDtypeStruct((B, d, S, LANES), F32)],
        compiler_params=_params(("parallel", "parallel", "arbitrary")),
        name=f"dilated_{dilation}",
    )(qkv, qkv, qkv)


def _outproj_kernel(x_ref, hf_ref, hb_ref, z_ref, b_ref, o0_ref, o1_ref, o2_ref, l0_ref, l1_ref, l2_ref,
                    gain_ref, w_ref, gpost_ref, out_ref, nat_ref, *, tm):
    z = z_ref[...].astype(F32)
    mo = z[:, :MLSTM_W]
    mz = z[:, MLSTM_W:2 * MLSTM_W]
    gz = z[:, 2 * MLSTM_W:2 * MLSTM_W + GQA_W]
    dz = z[:, 2 * MLSTM_W + GQA_W:]

    h = _sigmoid(mo) * (hf_ref[...] + hb_ref[...])
    a = h * lax.rsqrt(_head_sumsq(h) * (1.0 / HEAD_DIM) + NORM_EPS) * gain_ref[...] * _silu(mz)

    b = b_ref[...].astype(F32) * _silu(gz)

    n_groups = len(DIL_PATTERNS)
    for g, (o_ref, l_ref, (_, d)) in enumerate(zip((o0_ref, o1_ref, o2_ref), (l0_ref, l1_ref, l2_ref), DIL_PATTERNS)):
        for r in range(d):
            dst = slice(None) if d == 1 else pl.ds(r, tm // d, stride=d)
            nat_ref[g, dst, :] = o_ref[0, r].astype(F32)
            nat_ref[n_groups + g, dst, :] = l_ref[0, r]
    outs = [nat_ref[g] for g in range(n_groups)]
    lses = [nat_ref[n_groups + g] for g in range(n_groups)]
    mx = jnp.maximum(jnp.maximum(lses[0], lses[1]), lses[2])
    es = [jnp.exp(l - mx) for l in lses]
    inv = 1.0 / (es[0] + es[1] + es[2])
    c = jnp.concatenate([outs[g] * (es[g] * inv) for g in range(3)], axis=1) * _silu(dz)

    w = w_ref[...]
    y = jnp.dot(a.astype(BF16), w[:MLSTM_W], preferred_element_type=F32)
    y += jnp.dot(b.astype(BF16), w[MLSTM_W:MLSTM_W + GQA_W], preferred_element_type=F32)
    y += jnp.dot(c.astype(BF16), w[MLSTM_W + GQA_W:], preferred_element_type=F32)
    ms = jnp.mean(y * y, axis=-1, keepdims=True)
    out_ref[...] = x_ref[...] + y * lax.rsqrt(ms + NORM_EPS) * gpost_ref[...]


def _outproj(x, hf, hb, z, b, dil_o, dil_l, gain, w, gpost, T, tm=256):
    n = x.shape[0]
    tps = T // tm
    row = lambda a: pl.BlockSpec((tm, a.shape[1]), lambda i: (i, 0))
    full = lambda a: pl.BlockSpec(a.shape, lambda i: (0, 0))
    cls = lambda a: pl.BlockSpec((1, a.shape[1], tm // a.shape[1], LANES), lambda i: (i // tps, 0, i % tps, 0))
    rows = [x, hf, hb, z, b]
    dil = list(dil_o) + list(dil_l)
    consts = [gain, w, gpost]
    return pl.pallas_call(
        functools.partial(_outproj_kernel, tm=tm),
        grid=(n // tm,),
        in_specs=[row(a) for a in rows] + [cls(a) for a in dil] + [full(a) for a in consts],
        out_specs=pl.BlockSpec((tm, D_MODEL), lambda i: (i, 0)),
        out_shape=jax.ShapeDtypeStruct((n, D_MODEL), F32),
        scratch_shapes=[pltpu.VMEM((2 * len(DIL_PATTERNS), tm, LANES), F32)],
        compiler_params=_params(("parallel",)),
        name="outproj",
    )(*rows, *dil, *consts)


def _rope_tables(T):
    rows = T // GRID_W
    r, c = jnp.meshgrid(jnp.arange(rows), jnp.arange(GRID_W), indexing="ij")
    r = r.reshape(-1).astype(F32)
    c = c.reshape(-1).astype(F32)
    axis_dim = HEAD_DIM // 2
    inv = ROPE_THETA ** (-jnp.arange(0, axis_dim, 2, dtype=F32) / axis_dim)
    ang = jnp.concatenate([r[:, None] * inv, c[:, None] * inv], axis=-1)
    cos, sin = jnp.cos(ang), jnp.sin(ang)
    cos_t = jnp.tile(jnp.concatenate([cos, cos], axis=-1), (1, LANES // HEAD_DIM))
    sin_t = jnp.tile(jnp.concatenate([-sin, sin], axis=-1), (1, LANES // HEAD_DIM))
    return cos_t, sin_t


def _split_w_in(w):
    cuts = np.cumsum((0,) + IN_SPLITS)
    mq, mk, mv, mo, mg, mz, gq, gk, gv, gz, dq, dk, dv, dz = [w[:, cuts[i]:cuts[i + 1]] for i in range(14)]
    pad = jnp.zeros((w.shape[0], GATE_PAD - mg.shape[1]), w.dtype)
    cat = lambda *xs: jnp.concatenate(xs, axis=1).astype(BF16)
    return cat(mq, mk, mv, mg, pad), cat(gq, gk, gv), cat(dq, dk, dv), cat(mo, mz, gz, dz)


def _alibi_slopes():
    return [float(np.exp2(np.float32(-8.0) * np.float32(i) / np.float32(DIL_HEADS)))
            for i in range(1, DIL_HEADS + 1)]


def _layer(x, B, T, p, cos, sin, slopes):
    n = B * T
    m_qkv, m_gates, g1, *dil_qkv, z = _inproj(x, p["norm_pre"], *p["w_in"], p["conv_w"], p["conv_b"], B, T)
    hf, hb = _mlstm(m_qkv.reshape(B, T, 3 * MLSTM_W), m_gates.reshape(B, T, GATE_PAD), p["gate_bias"])
    b = _gqa(g1.reshape(B, T, G1_W), cos, sin, p["q_norm"], p["k_norm"])

    dil_o, dil_l = [], []
    for g, (_, dilation) in enumerate(DIL_PATTERNS):
        o_g, l_g = _dilated(dil_qkv[g], dilation, slopes[g * DIL_HEADS_PER_GROUP:(g + 1) * DIL_HEADS_PER_GROUP])
        dil_o.append(o_g)
        dil_l.append(l_g)

    return _outproj(x, hf.reshape(n, MLSTM_W), hb.reshape(n, MLSTM_W), z, b.reshape(n, GQA_W),
                    dil_o, dil_l, p["out_gain"], p["w_out"], p["norm_post"], T)


def _trunk(x, layers):
    B, T, _ = x.shape
    cos, sin = _rope_tables(T)
    slopes = _alibi_slopes()
    x = x.reshape(B * T, D_MODEL)
    for p in layers:
        x = _layer(x, B, T, p, cos, sin, slopes)
    return x.reshape(B, T, D_MODEL)


def kernel(x_prompt, x_sample, norm_pre, w_in, mlstm_gate_bias, mlstm_conv_w, mlstm_conv_b, mlstm_out_gain,
           gqa_q_norm, gqa_k_norm, w_out, norm_post):
    depth = w_in.shape[0]
    row = lambda a: a.reshape(1, -1).astype(F32)
    layers = []
    for l in range(depth):
        gate_bias = jnp.concatenate(
            [mlstm_gate_bias[l].astype(F32), jnp.zeros((GATE_PAD - 4 * MLSTM_HEADS,), F32)]).reshape(1, GATE_PAD)
        layers.append(dict(
            norm_pre=row(norm_pre[l]),
            w_in=_split_w_in(w_in[l]),
            gate_bias=gate_bias,
            conv_w=mlstm_conv_w[l].astype(F32),
            conv_b=row(mlstm_conv_b[l]),
            out_gain=row(mlstm_out_gain[l]),
            q_norm=row(jnp.tile(gqa_q_norm[l], GQA_Q_HEADS)),
            k_norm=row(jnp.tile(gqa_k_norm[l], GQA_KV_HEADS)),
            w_out=w_out[l].astype(BF16),
            norm_post=row(norm_post[l]),
        ))
    return (_trunk(x_prompt, layers), _trunk(x_sample, layers))
```

```python
import functools
import math

import numpy as np
import jax
import jax.numpy as jnp
from jax import lax
from jax.experimental import pallas as pl
from jax.experimental.pallas import tpu as pltpu

F32 = jnp.float32
BF16 = jnp.bfloat16

D_MODEL = 1024
HEAD_DIM = 64
NORM_EPS = 1e-6
GRID_W = 64
ROPE_THETA = 10000.0
MLSTM_HEADS = 4
MLSTM_W = MLSTM_HEADS * HEAD_DIM
MLSTM_CHUNK = 128
GQA_Q_HEADS = 6
GQA_KV_HEADS = 2
GQA_GROUP = GQA_Q_HEADS // GQA_KV_HEADS
GQA_W = GQA_Q_HEADS * HEAD_DIM
GQA_KV_W = GQA_KV_HEADS * HEAD_DIM
DIL_PATTERNS = ((128, 1), (512, 4), (2048, 16))
DIL_HEADS_PER_GROUP = 2
DIL_HEADS = DIL_HEADS_PER_GROUP * len(DIL_PATTERNS)
DIL_W = DIL_HEADS * HEAD_DIM
DIL_SIDE = 64
IN_SPLITS = (MLSTM_W, MLSTM_W, MLSTM_W, MLSTM_W, 4 * MLSTM_HEADS, MLSTM_W,
             GQA_W, GQA_KV_W, GQA_KV_W, GQA_W, DIL_W, DIL_W, DIL_W, DIL_W)

LANES = 128
SUBLANES = 8
GATE_PAD = LANES
M1_W = 3 * MLSTM_W + GATE_PAD
G1_W = GQA_W + 2 * GQA_KV_W
D1_W = 3 * DIL_W
Z_W = 2 * MLSTM_W + GQA_W + DIL_W
TOKEN_TILE = 512
VMEM_LIMIT = 48 * 1024 * 1024
LOG2E = math.log2(math.e)
NEG_INF = float("-inf")


def _sigmoid(x):
    return 1.0 / (1.0 + jnp.exp(-x))


def _silu(x):
    return x * _sigmoid(x)


def _log_sigmoid(x):
    return jnp.minimum(x, 0.0) - jnp.log(1.0 + jnp.exp(-jnp.abs(x)))


def _head_id(shape, axis):
    return lax.broadcasted_iota(jnp.int32, shape, axis) // HEAD_DIM


def _run_interleaved(stage_generators):
    live = list(stage_generators)
    while live:
        live = [g for g in live if next(g, StopIteration) is not StopIteration]


def _params(sem):
    return pltpu.CompilerParams(dimension_semantics=sem, vmem_limit_bytes=VMEM_LIMIT)


def _split_terms(a, terms):
    out = []
    for t in range(terms):
        hi = a.astype(BF16)
        out.append(hi)
        if t + 1 < terms:
            a = a - hi.astype(F32)
    return out


def _inproj_kernel(x_ref, xp_ref, xn_ref, g_ref, wm_ref, wg_ref, wd_ref, wz_ref, cw_ref, cb_ref,
                   cos_ref, sin_ref, qn_ref, kn_ref, nmb_ref,
                   oqkv_ref, ogate_ref, oqt_ref, ok_ref, ovt_ref, od1_ref, od4_ref, od16_ref, oz_ref, ds_ref,
                   *, tm, tps):
    def normed(x):
        ms = jnp.mean(x * x, axis=-1, keepdims=True)
        return (x * lax.rsqrt(ms + NORM_EPS) * g_ref[...]).astype(BF16)

    h = normed(x_ref[...])
    ym = jnp.dot(h, wm_ref[...], preferred_element_type=F32)
    qkw = 2 * MLSTM_W
    halo = normed(jnp.concatenate([xp_ref[...], xn_ref[...]], axis=0))
    yh = jnp.dot(halo, wm_ref[:, :qkw], preferred_element_type=F32)
    t = pl.program_id(0) % tps
    prev_row = yh[SUBLANES - 1:SUBLANES, :] * jnp.where(t > 0, 1.0, 0.0).astype(F32)
    next_row = yh[SUBLANES:SUBLANES + 1, :] * jnp.where(t < tps - 1, 1.0, 0.0).astype(F32)
    qk = ym[:, :qkw]
    rid = lax.broadcasted_iota(jnp.int32, (tm, qkw), 0)
    qk_m1 = jnp.where(rid == 0, prev_row, pltpu.roll(qk, 1, 0))
    qk_p1 = jnp.where(rid == tm - 1, next_row, pltpu.roll(qk, tm - 1, 0))
    cw = cw_ref[...]
    act = _silu(qk_m1 * cw[0:1, :] + qk * cw[1:2, :] + qk_p1 * cw[2:3, :] + cb_ref[...])
    oqkv_ref[:, :MLSTM_W] = act[:, :MLSTM_W].astype(BF16)
    oqkv_ref[:, MLSTM_W:qkw] = (act[:, MLSTM_W:] * HEAD_DIM ** -0.5).astype(BF16)
    oqkv_ref[:, qkw:] = ym[:, qkw:3 * MLSTM_W].astype(BF16)
    ogate_ref[...] = ym[:, 3 * MLSTM_W:]
    _gqa_prep_tile(jnp.dot(h, wg_ref[...], preferred_element_type=F32), cos_ref, sin_ref, qn_ref, kn_ref, nmb_ref,
                   oqt_ref, ok_ref, ovt_ref)
    oz_ref[...] = jnp.dot(h, wz_ref[...], preferred_element_type=F32).astype(BF16)
    yd = jnp.dot(h, wd_ref[...], preferred_element_type=F32)
    n_tiles = D1_W // LANES
    for j in range(n_tiles):
        ds_ref[j] = yd[:, j * LANES:(j + 1) * LANES]
    for g, (out_ref, (_, d)) in enumerate(zip((od1_ref, od4_ref, od16_ref), DIL_PATTERNS)):
        for part in range(3):
            src = part * len(DIL_PATTERNS) + g
            for r in range(d):
                rows = ds_ref[src] if d == 1 else ds_ref[src, pl.ds(r, tm // d, stride=d), :]
                out_ref[0, r, :, part * LANES:(part + 1) * LANES] = rows.astype(BF16)


def _inproj(x, g, wm, wg, wd, wz, conv_w, conv_b, cos, sin, q_norm, k_norm, neg_mb, B, T, tm):
    n = x.shape[0]
    tps = T // tm
    hb = tm // SUBLANES
    row = lambda w: pl.BlockSpec((tm, w), lambda i: (i, 0))
    full = lambda a: pl.BlockSpec(a.shape, lambda i: (0, 0))
    pos = pl.BlockSpec((tm, LANES), lambda i: (i % tps, 0))
    gqa_specs = [pl.BlockSpec((1, GQA_KV_HEADS, 1, LANES, GQA_GROUP * tm), lambda i: (i // tps, 0, i % tps, 0, 0)),
                 pl.BlockSpec((1, GQA_KV_HEADS, tm, LANES), lambda i: (i // tps, 0, i % tps, 0)),
                 pl.BlockSpec((1, GQA_KV_HEADS, GQA_VT_ROWS, tm), lambda i: (i // tps, 0, 0, i % tps))]
    gqa_shapes = [jax.ShapeDtypeStruct((B, GQA_KV_HEADS, tps, LANES, GQA_GROUP * tm), BF16),
                  jax.ShapeDtypeStruct((B, GQA_KV_HEADS, T, LANES), BF16),
                  jax.ShapeDtypeStruct((B, GQA_KV_HEADS, GQA_VT_ROWS, T), BF16)]
    dil_spec = lambda d: pl.BlockSpec((1, d, tm // d, 3 * LANES), lambda i: (i // tps, 0, i % tps, 0))
    dil_shape = lambda d: jax.ShapeDtypeStruct((B, d, T // d, 3 * LANES), BF16)
    dils = [d for _, d in DIL_PATTERNS]
    halo_prev = pl.BlockSpec((SUBLANES, D_MODEL), lambda i: (jnp.maximum(i * hb - 1, 0), 0))
    halo_next = pl.BlockSpec((SUBLANES, D_MODEL), lambda i: (jnp.minimum((i + 1) * hb, n // SUBLANES - 1), 0))
    return pl.pallas_call(
        functools.partial(_inproj_kernel, tm=tm, tps=tps),
        grid=(n // tm,),
        in_specs=[row(D_MODEL), halo_prev, halo_next, full(g), full(wm), full(wg), full(wd), full(wz),
                  full(conv_w), full(conv_b), pos, pos, full(q_norm), full(k_norm), full(neg_mb)],
        out_specs=[row(3 * MLSTM_W), row(GATE_PAD)] + gqa_specs + [dil_spec(d) for d in dils] + [row(Z_W)],
        out_shape=[jax.ShapeDtypeStruct((n, 3 * MLSTM_W), BF16), jax.ShapeDtypeStruct((n, GATE_PAD), F32)]
        + gqa_shapes + [dil_shape(d) for d in dils] + [jax.ShapeDtypeStruct((n, Z_W), BF16)],
        scratch_shapes=[pltpu.VMEM((D1_W // LANES, tm, LANES), F32)],
        compiler_params=_params(("parallel",)),
        name="inproj",
    )(x, x, x, g, wm, wg, wd, wz, conv_w, conv_b, cos, sin, q_norm, k_norm, neg_mb)


MLSTM_EXP2_CAP = 126.0


def _mlstm_consts():
    L, W, H = MLSTM_CHUNK, MLSTM_W, MLSTM_HEADS
    r = np.arange(L)
    tri = np.stack([r[None, :] <= r[:, None], r[None, :] >= r[:, None]]).astype(np.float32)
    lane_head = np.arange(W) // HEAD_DIM
    spread = np.zeros((2, GATE_PAD, W), np.float32)
    neg = np.full((SUBLANES, GATE_PAD), NEG_INF, np.float32)
    for d in range(2):
        fo = (2 * d + 1) * H
        for h in range(H):
            spread[d, fo + h] = lane_head == h
        neg[d, fo:fo + H] = 0.0
    ones_stack = (np.repeat(np.arange(H), L)[:, None] == lane_head[None, :]).astype(np.float32)
    same_head = (lane_head[:, None] == lane_head[None, :]).astype(np.float32)
    return (jnp.asarray(tri, BF16), jnp.asarray(spread, BF16), jnp.asarray(neg), jnp.asarray(ones_stack, BF16),
            jnp.asarray(same_head, BF16), jnp.asarray(same_head))


def _mlstm_direction(qkv_ref, gate_ref, out_ref, c_ref, n_ref, m_ref, gb_ref, tri_ref, spread_ref, neg_ref,
                     os_ref, shb_ref, shf_ref, bi, d):
    forward = d == 0
    st = 2 * bi + d
    L, W, H = MLSTM_CHUNK, MLSTM_W, MLSTM_HEADS
    x = qkv_ref[bi]
    q, k, v = x[:, :W], x[:, W:2 * W], x[:, 2 * W:]
    g = gate_ref[bi] + gb_ref[...]
    fo = (2 * d + 1) * H
    c_state = c_ref[st]
    n_state = n_ref[st][0:1, :]
    s_heads = [lax.dot_general(q * os_ref[h * L:(h + 1) * L, :], k, (((1,), (1,)), ((), ())),
                               preferred_element_type=F32) for h in range(H)]
    q_c = jnp.dot(q, c_state.astype(BF16), preferred_element_type=F32)
    q_n = jnp.dot((q.astype(F32) * n_state).astype(BF16), shb_ref[...], preferred_element_type=F32)
    cum = sum(jnp.dot(tri_ref[d], t, preferred_element_type=F32)
              for t in _split_terms(_log_sigmoid(g) * LOG2E, 3))
    yield
    u = pltpu.roll(g * LOG2E, H, 1) - cum
    u_t = u.T
    r_i = lax.broadcasted_iota(jnp.int32, (L, L), 0)
    c_i = lax.broadcasted_iota(jnp.int32, (L, L), 1)
    seen = (c_i <= r_i) if forward else (c_i >= r_i)
    gl = lax.broadcasted_iota(jnp.int32, (L, GATE_PAD), 1)
    m_state = m_ref[st][0:1, :]
    m_row = jnp.zeros((L, GATE_PAD), F32)
    w_blocks = []
    for h in range(H):
        col = fo + h
        bc = cum[:, col:col + 1]
        dmat = jnp.where(seen, bc + u_t[col:col + 1, :], NEG_INF)
        m_row_h = jnp.maximum(bc + m_state[:, col:col + 1], jnp.max(dmat, axis=1, keepdims=True))
        w_blocks.append((jnp.exp2(dmat - m_row_h) * s_heads[h]).astype(BF16))
        m_row = jnp.where(gl == col, m_row_h, m_row)

    neg = neg_ref[d:d + 1, :]
    tot_row = L - 1 if forward else 0
    b_tot = cum[tot_row:tot_row + 1, :]
    gv = b_tot + u
    m_new = jnp.maximum(b_tot + m_state, jnp.max(gv, axis=0, keepdims=True))
    decay = jnp.exp2(b_tot + m_state - m_new + neg)
    stack = jnp.concatenate([jnp.exp2(cum + m_state - m_row + neg),
                             jnp.exp2(jnp.minimum(-m_row, MLSTM_EXP2_CAP) + neg),
                             jnp.exp2(gv - m_new + neg),
                             jnp.broadcast_to(decay, (SUBLANES, GATE_PAD))], axis=0)
    spread = sum(jnp.dot(t, spread_ref[d], preferred_element_type=F32) for t in _split_terms(stack, 2))
    w_inter, inv_floor, k_fac, decay_e = spread[:L], spread[L:2 * L], spread[2 * L:3 * L], spread[3 * L:3 * L + 1]
    yield
    w_cat = jnp.concatenate(w_blocks, axis=1)
    v_stack = os_ref[...] * jnp.concatenate([v] * H, axis=0)
    num = w_inter * q_c + jnp.dot(w_cat, v_stack, preferred_element_type=F32)
    den = w_inter * q_n + jnp.dot(w_cat, os_ref[...], preferred_element_type=F32)
    out_ref[bi] = num / jnp.maximum(jnp.abs(den), inv_floor)
    yield
    kw = k_fac.astype(BF16) * k
    kv = lax.dot_general(kw, v, (((0,), (0,)), ((), ())), preferred_element_type=F32)
    k_sum = jnp.dot(jnp.ones((2 * SUBLANES, L), BF16), kw, preferred_element_type=F32)[0:1, :]
    c_ref[st] = decay_e * c_state + jnp.where(shf_ref[...] != 0.0, kv, 0.0)
    n_ref[st] = jnp.broadcast_to(decay_e * n_state + k_sum, (SUBLANES, W))
    m_ref[st] = jnp.broadcast_to(jnp.where(neg == 0.0, m_new, 0.0), (SUBLANES, GATE_PAD))


def _mlstm_kernel(qf_ref, gf_ref, qb_ref, gbk_ref, gb_ref, tri_ref, spread_ref, neg_ref, os_ref, shb_ref, shf_ref,
                  hf_ref, hb_ref, c_ref, n_ref, m_ref, *, bb):
    @pl.when(pl.program_id(1) == 0)
    def _():
        c_ref[...] = jnp.zeros_like(c_ref)
        n_ref[...] = jnp.zeros_like(n_ref)
        m_ref[...] = jnp.zeros_like(m_ref)

    consts = (gb_ref, tri_ref, spread_ref, neg_ref, os_ref, shb_ref, shf_ref)
    chains = []
    for bi in range(bb):
        chains.append(_mlstm_direction(qf_ref, gf_ref, hf_ref, c_ref, n_ref, m_ref, *consts, bi, 0))
        chains.append(_mlstm_direction(qb_ref, gbk_ref, hb_ref, c_ref, n_ref, m_ref, *consts, bi, 1))
    _run_interleaved(chains)


def _mlstm(qkv, gates, gate_bias):
    B, T, _ = qkv.shape
    L = MLSTM_CHUNK
    nc = T // L
    bb = 2 if B % 2 == 0 else 1
    consts = (gate_bias,) + _mlstm_consts()
    full = lambda a: pl.BlockSpec(a.shape, lambda b, c: (0,) * a.ndim)
    chunk = lambda w, pos: pl.BlockSpec((bb, L, w), lambda b, c: (b, pos(c), 0))
    fwd = lambda c: c
    bwd = lambda c: nc - 1 - c
    return pl.pallas_call(
        functools.partial(_mlstm_kernel, bb=bb),
        grid=(B // bb, nc),
        in_specs=[chunk(3 * MLSTM_W, fwd), chunk(GATE_PAD, fwd), chunk(3 * MLSTM_W, bwd), chunk(GATE_PAD, bwd)]
        + [full(a) for a in consts],
        out_specs=[chunk(MLSTM_W, fwd), chunk(MLSTM_W, bwd)],
        out_shape=[jax.ShapeDtypeStruct((B, T, MLSTM_W), F32)] * 2,
        scratch_shapes=[pltpu.VMEM((2 * bb, MLSTM_W, MLSTM_W), F32),
                        pltpu.VMEM((2 * bb, SUBLANES, MLSTM_W), F32),
                        pltpu.VMEM((2 * bb, SUBLANES, GATE_PAD), F32)],
        compiler_params=_params(("parallel", "arbitrary")),
        name="mlstm",
    )(qkv, gates, qkv, gates, *consts)


GQA_SAFE_LOG2 = 56.0
GQA_BOUND_SLACK = 1.0 + 2.0 ** -6
GQA_SCALE = HEAD_DIM ** -0.5 * LOG2E
GQA_VT_ROWS = HEAD_DIM + 16


def _rope_tile(a, cos, sin):
    half = HEAD_DIM // 2
    lane = lax.broadcasted_iota(jnp.int32, a.shape, 1) % HEAD_DIM
    rot = jnp.where(lane < half, pltpu.roll(a, LANES - half, 1), pltpu.roll(a, half, 1))
    return a * cos + rot * sin


def _head_sumsq(a):
    w = a.shape[1]
    same_head = jnp.where(_head_id((w, w), 0) == _head_id((w, w), 1), 1.0, 0.0).astype(BF16)
    return sum(jnp.dot(t, same_head, preferred_element_type=F32) for t in _split_terms(a * a, 2))


def _head_rms(a, gain):
    return a * lax.rsqrt(_head_sumsq(a) * (1.0 / HEAD_DIM) + NORM_EPS) * gain


def _gqa_prep_tile(x, cos_ref, sin_ref, qn_ref, kn_ref, nmb_ref, qt_out, k_out, vt_out):
    cos = cos_ref[...]
    sin = sin_ref[...]
    q = _head_rms(x[:, :GQA_W], qn_ref[...])
    k = _head_rms(x[:, GQA_W:GQA_W + GQA_KV_W], kn_ref[...])
    v = x[:, GQA_W + GQA_KV_W:]
    lane = lax.broadcasted_iota(jnp.int32, (x.shape[0], LANES), 1)
    first = lane < HEAD_DIM
    bound_col = jnp.where(lane == HEAD_DIM, nmb_ref[...], 0.0)
    ones_col = jnp.where(lane == HEAD_DIM, 1.0, 0.0)
    q_t = []
    for j in range(GQA_W // LANES):
        pair = _rope_tile(q[:, j * LANES:(j + 1) * LANES], cos, sin) * GQA_SCALE
        q_t.append(jnp.where(first, pair, bound_col).T.astype(BF16))
        q_t.append(jnp.where(first, pltpu.roll(pair, HEAD_DIM, 1), bound_col).T.astype(BF16))
    for kk in range(GQA_KV_HEADS):
        qt_out[0, kk, 0] = jnp.concatenate(q_t[kk * GQA_GROUP:(kk + 1) * GQA_GROUP], axis=1)
    k_rot = _rope_tile(k, cos, sin)
    k_out[0, 0] = jnp.where(first, k_rot, ones_col).astype(BF16)
    k_out[0, 1] = jnp.where(first, pltpu.roll(k_rot, HEAD_DIM, 1), ones_col).astype(BF16)
    vt_out[0, 0] = jnp.where(first, v, 1.0).T[:GQA_VT_ROWS].astype(BF16)
    vt_out[0, 1] = jnp.where(first, pltpu.roll(v, HEAD_DIM, 1), 1.0).T[:GQA_VT_ROWS].astype(BF16)


def _gqa_bound(q_norm, k_norm):
    return (HEAD_DIM * GQA_SCALE * GQA_BOUND_SLACK) * jnp.max(jnp.abs(q_norm)) * jnp.max(jnp.abs(k_norm))


def _gqa_finish(o_ref, acc_t, kh, tq):
    o_t = acc_t[:HEAD_DIM, :] / acc_t[HEAD_DIM:HEAD_DIM + 1, :]
    o = jnp.concatenate([o_t, o_t], axis=0).T
    for kk in range(GQA_KV_HEADS):
        @pl.when(kh == kk)
        def _():
            for h in range(GQA_GROUP):
                lo = (kk * GQA_GROUP + h) * HEAD_DIM
                o_ref[0, :, lo:lo + HEAD_DIM] = o[h * tq:(h + 1) * tq, :HEAD_DIM].astype(o_ref.dtype)


def _gqa_fast_kernel(qt_ref, k_ref, vt_ref, o_ref, acc_ref, *, tq, tk, kc, nk):
    kh = pl.program_id(2)
    ki = pl.program_id(3)

    @pl.when(ki == 0)
    def _():
        acc_ref[...] = jnp.zeros_like(acc_ref)

    qt = qt_ref[0, 0, 0]
    scores = lambda c: jnp.dot(k_ref[0, 0, c * kc:(c + 1) * kc, :], qt, preferred_element_type=F32)
    n_chunks = tk // kc
    acc = None
    st_next = scores(0)
    for c in range(n_chunks):
        st = st_next
        if c + 1 < n_chunks:
            st_next = scores(c + 1)
        pt = jnp.exp2(st).astype(BF16)
        d = jnp.dot(vt_ref[0, 0, :, c * kc:(c + 1) * kc], pt, preferred_element_type=F32)
        acc = d if acc is None else acc + d
    acc_ref[...] += acc

    @pl.when(ki == nk - 1)
    def _():
        _gqa_finish(o_ref, acc_ref[...], kh, tq)


def _gqa_online_kernel(qt_ref, k_ref, vt_ref, o_ref, m_ref, acc_ref, *, tq, nk):
    kh = pl.program_id(2)
    ki = pl.program_id(3)

    @pl.when(ki == 0)
    def _():
        m_ref[...] = jnp.full_like(m_ref, NEG_INF)
        acc_ref[...] = jnp.zeros_like(acc_ref)

    st = jnp.dot(k_ref[0, 0][:, :HEAD_DIM], qt_ref[0, 0, 0][:HEAD_DIM, :],
                 preferred_element_type=F32)
    m_prev = m_ref[0:1, :]
    m_new = jnp.maximum(m_prev, jnp.max(st, axis=0, keepdims=True))
    alpha = jnp.exp2(m_prev - m_new)
    pt = jnp.exp2(st - m_new).astype(BF16)
    acc_ref[...] = alpha * acc_ref[...] + jnp.dot(vt_ref[0, 0], pt, preferred_element_type=F32)
    m_ref[...] = jnp.broadcast_to(m_new, m_ref.shape)

    @pl.when(ki == nk - 1)
    def _():
        _gqa_finish(o_ref, acc_ref[...], kh, tq)


def _gqa_attend(qt, k, vt, fast, tq):
    B, _, T, _ = k.shape
    tk = min(4096, T) if fast else 512
    nq, nk = T // tq, T // tk
    n_cols = GQA_GROUP * tq
    acc = pltpu.VMEM((GQA_VT_ROWS, n_cols), F32)
    if fast:
        body = functools.partial(_gqa_fast_kernel, tq=tq, tk=tk, kc=256, nk=nk)
        scratch = [acc]
    else:
        body = functools.partial(_gqa_online_kernel, tq=tq, nk=nk)
        scratch = [pltpu.VMEM((SUBLANES, n_cols), F32), acc]
    return pl.pallas_call(
        body,
        grid=(B, nq, GQA_KV_HEADS, nk),
        in_specs=[pl.BlockSpec((1, 1, 1, LANES, n_cols), lambda b, i, h, j: (b, h, i, 0, 0)),
                  pl.BlockSpec((1, 1, tk, LANES), lambda b, i, h, j: (b, h, j, 0)),
                  pl.BlockSpec((1, 1, GQA_VT_ROWS, tk), lambda b, i, h, j: (b, h, 0, j))],
        out_specs=pl.BlockSpec((1, tq, GQA_W), lambda b, i, h, j: (b, i, 0)),
        out_shape=jax.ShapeDtypeStruct((B, T, GQA_W), BF16),
        scratch_shapes=scratch,
        compiler_params=_params(("parallel", "arbitrary", "arbitrary", "arbitrary")),
        name="gqa_fast" if fast else "gqa_online",
    )(qt, k, vt)


def _gqa(qt, k, vt, mb, tq):
    return lax.cond(mb <= GQA_SAFE_LOG2,
                    functools.partial(_gqa_attend, fast=True, tq=tq),
                    functools.partial(_gqa_attend, fast=False, tq=tq), qt, k, vt)


DIL_UNITS = 8


def _dil_kernel(q_ref, k_ref, v_ref, o_ref, lse_ref, *, S, bq, kw, nb, rb, dilation, slopes):
    base = pl.program_id(2) * (nb * bq)
    row = lax.broadcasted_iota(jnp.int32, (bq, kw), 0)
    col = lax.broadcasted_iota(jnp.int32, (bq, kw), 1)
    hid = _head_id((bq, LANES), 1)

    def unit(rr, n):
        p0 = base + n * bq
        if kw == S:
            start = 0
            kwin, vwin = k_ref[0, rr], v_ref[0, rr]
        else:
            start = pl.multiple_of(jnp.clip(p0 - DIL_SIDE, 0, S - kw), DIL_SIDE)
            kwin = k_ref[0, rr, pl.ds(start, kw), :]
            vwin = v_ref[0, rr, pl.ds(start, kw), :]
        q = q_ref[0, rr, n * bq:(n + 1) * bq, :]
        scores = [lax.dot_general(jnp.where(hid == hh, q, jnp.zeros_like(q)), kwin, (((1,), (1,)), ((), ())),
                                  preferred_element_type=F32) for hh in range(DIL_HEADS_PER_GROUP)]
        yield
        rel = jnp.abs(col - row + (start - p0))
        valid = rel <= DIL_SIDE
        dist = (rel * dilation).astype(F32)
        outs, lses = [], []
        for hh in range(DIL_HEADS_PER_GROUP):
            s = jnp.where(valid, scores[hh] * HEAD_DIM ** -0.5 - slopes[hh] * dist, NEG_INF)
            m = jnp.max(s, axis=-1, keepdims=True)
            p = jnp.exp(s - m)
            l = jnp.sum(p, axis=-1, keepdims=True)
            outs.append(jnp.dot(p.astype(BF16), vwin, preferred_element_type=F32) / l)
            lses.append(m + jnp.log(l))
        o_ref[0, rr, n * bq:(n + 1) * bq, :] = jnp.where(hid == 0, outs[0], outs[1]).astype(o_ref.dtype)
        lse_ref[0, rr, n * bq:(n + 1) * bq, :] = jnp.where(hid == 0, lses[0], lses[1])

    _run_interleaved([unit(rr, n) for rr in range(rb) for n in range(nb)])


def _dilated(qkv, dilation, slopes):
    B, d, S, _ = qkv.shape
    bq = min(128, S)
    kw = min(bq + 2 * DIL_SIDE, S)
    nb = min(DIL_UNITS, S // bq)
    rb = min(d, DIL_UNITS // nb)
    seq = lambda part: pl.BlockSpec((1, rb, S, LANES), lambda b, r, i: (b, r, 0, part))
    blk = lambda part: pl.BlockSpec((1, rb, nb * bq, LANES), lambda b, r, i: (b, r, i, part))
    return pl.pallas_call(
        functools.partial(_dil_kernel, S=S, bq=bq, kw=kw, nb=nb, rb=rb, dilation=dilation, slopes=slopes),
        grid=(B, d // rb, S // (nb * bq)),
        in_specs=[blk(0), seq(1), seq(2)],
        out_specs=[blk(0), blk(0)],
        out_shape=[jax.ShapeDtypeStruct((B, d, S, LANES), BF16), jax.ShapeDtypeStruct((B, d, S, LANES), F32)],
        compiler_params=_params(("parallel", "parallel", "arbitrary")),
        name=f"dilated_{dilation}",
    )(qkv, qkv, qkv)


def _outproj_kernel(x_ref, hf_ref, hb_ref, z_ref, b_ref, o0_ref, o1_ref, o2_ref, l0_ref, l1_ref, l2_ref,
                    gain_ref, w_ref, gpost_ref, out_ref, nat_ref, *, tm):
    z = z_ref[...].astype(F32)
    mo = z[:, :MLSTM_W]
    mz = z[:, MLSTM_W:2 * MLSTM_W]
    gz = z[:, 2 * MLSTM_W:2 * MLSTM_W + GQA_W]
    dz = z[:, 2 * MLSTM_W + GQA_W:]

    h = _sigmoid(mo) * (hf_ref[...] + hb_ref[...])
    a = h * lax.rsqrt(_head_sumsq(h) * (1.0 / HEAD_DIM) + NORM_EPS) * gain_ref[...] * _silu(mz)

    b = b_ref[...].astype(F32) * _silu(gz)

    n_groups = len(DIL_PATTERNS)
    for g, (o_ref, l_ref, (_, d)) in enumerate(zip((o0_ref, o1_ref, o2_ref), (l0_ref, l1_ref, l2_ref), DIL_PATTERNS)):
        for r in range(d):
            dst = slice(None) if d == 1 else pl.ds(r, tm // d, stride=d)
            nat_ref[g, dst, :] = o_ref[0, r].astype(F32)
            nat_ref[n_groups + g, dst, :] = l_ref[0, r]
    outs = [nat_ref[g] for g in range(n_groups)]
    lses = [nat_ref[n_groups + g] for g in range(n_groups)]
    mx = jnp.maximum(jnp.maximum(lses[0], lses[1]), lses[2])
    es = [jnp.exp(l - mx) for l in lses]
    inv = 1.0 / (es[0] + es[1] + es[2])
    c = jnp.concatenate([outs[g] * (es[g] * inv) for g in range(3)], axis=1) * _silu(dz)

    w = w_ref[...]
    y = jnp.dot(a.astype(BF16), w[:MLSTM_W], preferred_element_type=F32)
    y += jnp.dot(b.astype(BF16), w[MLSTM_W:MLSTM_W + GQA_W], preferred_element_type=F32)
    y += jnp.dot(c.astype(BF16), w[MLSTM_W + GQA_W:], preferred_element_type=F32)
    ms = jnp.mean(y * y, axis=-1, keepdims=True)
    out_ref[...] = x_ref[...] + y * lax.rsqrt(ms + NORM_EPS) * gpost_ref[...]


def _outproj(x, hf, hb, z, b, dil_o, dil_l, gain, w, gpost, T, tm=256):
    n = x.shape[0]
    tps = T // tm
    row = lambda a: pl.BlockSpec((tm, a.shape[1]), lambda i: (i, 0))
    full = lambda a: pl.BlockSpec(a.shape, lambda i: (0, 0))
    cls = lambda a: pl.BlockSpec((1, a.shape[1], tm // a.shape[1], LANES), lambda i: (i // tps, 0, i % tps, 0))
    rows = [x, hf, hb, z, b]
    dil = list(dil_o) + list(dil_l)
    consts = [gain, w, gpost]
    return pl.pallas_call(
        functools.partial(_outproj_kernel, tm=tm),
        grid=(n // tm,),
        in_specs=[row(a) for a in rows] + [cls(a) for a in dil] + [full(a) for a in consts],
        out_specs=pl.BlockSpec((tm, D_MODEL), lambda i: (i, 0)),
        out_shape=jax.ShapeDtypeStruct((n, D_MODEL), F32),
        scratch_shapes=[pltpu.VMEM((2 * len(DIL_PATTERNS), tm, LANES), F32)],
        compiler_params=_params(("parallel",)),
        name="outproj",
    )(*rows, *dil, *consts)


def _rope_tables(T):
    rows = T // GRID_W
    r, c = jnp.meshgrid(jnp.arange(rows), jnp.arange(GRID_W), indexing="ij")
    r = r.reshape(-1).astype(F32)
    c = c.reshape(-1).astype(F32)
    axis_dim = HEAD_DIM // 2
    inv = ROPE_THETA ** (-jnp.arange(0, axis_dim, 2, dtype=F32) / axis_dim)
    ang = jnp.concatenate([r[:, None] * inv, c[:, None] * inv], axis=-1)
    cos, sin = jnp.cos(ang), jnp.sin(ang)
    cos_t = jnp.tile(jnp.concatenate([cos, cos], axis=-1), (1, LANES // HEAD_DIM))
    sin_t = jnp.tile(jnp.concatenate([-sin, sin], axis=-1), (1, LANES // HEAD_DIM))
    return cos_t, sin_t


def _split_w_in(w):
    cuts = np.cumsum((0,) + IN_SPLITS)
    mq, mk, mv, mo, mg, mz, gq, gk, gv, gz, dq, dk, dv, dz = [w[:, cuts[i]:cuts[i + 1]] for i in range(14)]
    pad = jnp.zeros((w.shape[0], GATE_PAD - mg.shape[1]), w.dtype)
    cat = lambda *xs: jnp.concatenate(xs, axis=1).astype(BF16)
    return cat(mq, mk, mv, mg, pad), cat(gq, gk, gv), cat(dq, dk, dv), cat(mo, mz, gz, dz)


def _alibi_slopes():
    return [float(np.exp2(np.float32(-8.0) * np.float32(i) / np.float32(DIL_HEADS)))
            for i in range(1, DIL_HEADS + 1)]


def _layer(x, B, T, p, cos, sin, slopes):
    n = B * T
    mb = _gqa_bound(p["q_norm"], p["k_norm"])
    m_qkv, m_gates, qt, k, vt, *dil_qkv, z = _inproj(
        x, p["norm_pre"], *p["w_in"], p["conv_w"], p["conv_b"], cos, sin, p["q_norm"], p["k_norm"],
        jnp.full((1, LANES), -mb, F32), B, T, TOKEN_TILE)
    hf, hb = _mlstm(m_qkv.reshape(B, T, 3 * MLSTM_W), m_gates.reshape(B, T, GATE_PAD), p["gate_bias"])
    b = _gqa(qt, k, vt, mb, TOKEN_TILE)

    dil_o, dil_l = [], []
    for g, (_, dilation) in enumerate(DIL_PATTERNS):
        o_g, l_g = _dilated(dil_qkv[g], dilation, slopes[g * DIL_HEADS_PER_GROUP:(g + 1) * DIL_HEADS_PER_GROUP])
        dil_o.append(o_g)
        dil_l.append(l_g)

    return _outproj(x, hf.reshape(n, MLSTM_W), hb.reshape(n, MLSTM_W), z, b.reshape(n, GQA_W),
                    dil_o, dil_l, p["out_gain"], p["w_out"], p["norm_post"], T)


def _trunk(x, layers):
    B, T, _ = x.shape
    cos, sin = _rope_tables(T)
    slopes = _alibi_slopes()
    x = x.reshape(B * T, D_MODEL)
    for p in layers:
        x = _layer(x, B, T, p, cos, sin, slopes)
    return x.reshape(B, T, D_MODEL)


def kernel(x_prompt, x_sample, norm_pre, w_in, mlstm_gate_bias, mlstm_conv_w, mlstm_conv_b, mlstm_out_gain,
           gqa_q_norm, gqa_k_norm, w_out, norm_post):
    depth = w_in.shape[0]
    row = lambda a: a.reshape(1, -1).astype(F32)
    layers = []
    for l in range(depth):
        gate_bias = jnp.concatenate(
            [mlstm_gate_bias[l].astype(F32), jnp.zeros((GATE_PAD - 4 * MLSTM_HEADS,), F32)]).reshape(1, GATE_PAD)
        layers.append(dict(
            norm_pre=row(norm_pre[l]),
            w_in=_split_w_in(w_in[l]),
            gate_bias=gate_bias,
            conv_w=mlstm_conv_w[l].astype(F32),
            conv_b=row(mlstm_conv_b[l]),
            out_gain=row(mlstm_out_gain[l]),
            q_norm=row(jnp.tile(gqa_q_norm[l], GQA_Q_HEADS)),
            k_norm=row(jnp.tile(gqa_k_norm[l], GQA_KV_HEADS)),
            w_out=w_out[l].astype(BF16),
            norm_post=row(norm_post[l]),
        ))
    return (_trunk(x_prompt, layers), _trunk(x_sample, layers))
```

```python
import functools
import math

import numpy as np
import jax
import jax.numpy as jnp
from jax import lax
from jax.experimental import pallas as pl
from jax.experimental.pallas import tpu as pltpu

F32 = jnp.float32
BF16 = jnp.bfloat16

D_MODEL = 1024
HEAD_DIM = 64
NORM_EPS = 1e-6
GRID_W = 64
ROPE_THETA = 10000.0
MLSTM_HEADS = 4
MLSTM_W = MLSTM_HEADS * HEAD_DIM
MLSTM_CHUNK = 128
GQA_Q_HEADS = 6
GQA_KV_HEADS = 2
GQA_GROUP = GQA_Q_HEADS // GQA_KV_HEADS
GQA_W = GQA_Q_HEADS * HEAD_DIM
GQA_KV_W = GQA_KV_HEADS * HEAD_DIM
DIL_PATTERNS = ((128, 1), (512, 4), (2048, 16))
DIL_HEADS_PER_GROUP = 2
DIL_HEADS = DIL_HEADS_PER_GROUP * len(DIL_PATTERNS)
DIL_W = DIL_HEADS * HEAD_DIM
DIL_SIDE = 64
IN_SPLITS = (MLSTM_W, MLSTM_W, MLSTM_W, MLSTM_W, 4 * MLSTM_HEADS, MLSTM_W,
             GQA_W, GQA_KV_W, GQA_KV_W, GQA_W, DIL_W, DIL_W, DIL_W, DIL_W)

LANES = 128
SUBLANES = 8
GATE_PAD = LANES
M1_W = 3 * MLSTM_W + GATE_PAD
G1_W = GQA_W + 2 * GQA_KV_W
D1_W = 3 * DIL_W
Z_W = 2 * MLSTM_W + GQA_W + DIL_W
TOKEN_TILE = 512
VMEM_LIMIT = 48 * 1024 * 1024
LOG2E = math.log2(math.e)
NEG_INF = float("-inf")


def _sigmoid(x):
    return 1.0 / (1.0 + jnp.exp(-x))


def _silu(x):
    return x * _sigmoid(x)


def _log_sigmoid(x):
    return jnp.minimum(x, 0.0) - jnp.log(1.0 + jnp.exp(-jnp.abs(x)))


def _head_id(shape, axis):
    return lax.broadcasted_iota(jnp.int32, shape, axis) // HEAD_DIM


def _run_interleaved(stage_generators):
    live = list(stage_generators)
    while live:
        live = [g for g in live if next(g, StopIteration) is not StopIteration]


def _params(sem):
    return pltpu.CompilerParams(dimension_semantics=sem, vmem_limit_bytes=VMEM_LIMIT)


def _split_terms(a, terms):
    out = []
    for t in range(terms):
        hi = a.astype(BF16)
        out.append(hi)
        if t + 1 < terms:
            a = a - hi.astype(F32)
    return out


def _inproj_kernel(x_ref, xp_ref, xn_ref, g_ref, wm_ref, wg_ref, wd_ref, wz_ref, cw_ref, cb_ref,
                   cos_ref, sin_ref, qn_ref, kn_ref, nmb_ref,
                   oqkv_ref, ogate_ref, oqt_ref, ok_ref, ovt_ref, od1_ref, od4_ref, od16_ref, oz_ref, ds_ref,
                   *, tm, tps):
    def normed(x):
        ms = jnp.mean(x * x, axis=-1, keepdims=True)
        return (x * lax.rsqrt(ms + NORM_EPS) * g_ref[...]).astype(BF16)

    h = normed(x_ref[...])
    ym = jnp.dot(h, wm_ref[...], preferred_element_type=F32)
    qkw = 2 * MLSTM_W
    halo = normed(jnp.concatenate([xp_ref[...], xn_ref[...]], axis=0))
    yh = jnp.dot(halo, wm_ref[:, :qkw], preferred_element_type=F32)
    t = pl.program_id(0) % tps
    prev_row = yh[SUBLANES - 1:SUBLANES, :] * jnp.where(t > 0, 1.0, 0.0).astype(F32)
    next_row = yh[SUBLANES:SUBLANES + 1, :] * jnp.where(t < tps - 1, 1.0, 0.0).astype(F32)
    qk = ym[:, :qkw]
    rid = lax.broadcasted_iota(jnp.int32, (tm, qkw), 0)
    qk_m1 = jnp.where(rid == 0, prev_row, pltpu.roll(qk, 1, 0))
    qk_p1 = jnp.where(rid == tm - 1, next_row, pltpu.roll(qk, tm - 1, 0))
    cw = cw_ref[...]
    act = _silu(qk_m1 * cw[0:1, :] + qk * cw[1:2, :] + qk_p1 * cw[2:3, :] + cb_ref[...])
    oqkv_ref[:, :MLSTM_W] = act[:, :MLSTM_W].astype(BF16)
    oqkv_ref[:, MLSTM_W:qkw] = (act[:, MLSTM_W:] * HEAD_DIM ** -0.5).astype(BF16)
    oqkv_ref[:, qkw:] = ym[:, qkw:3 * MLSTM_W].astype(BF16)
    ogate_ref[...] = ym[:, 3 * MLSTM_W:]
    _gqa_prep_tile(jnp.dot(h, wg_ref[...], preferred_element_type=F32), cos_ref, sin_ref, qn_ref, kn_ref, nmb_ref,
                   oqt_ref, ok_ref, ovt_ref)
    oz_ref[...] = jnp.dot(h, wz_ref[...], preferred_element_type=F32).astype(BF16)
    yd = jnp.dot(h, wd_ref[...], preferred_element_type=F32)
    n_tiles = D1_W // LANES
    for j in range(n_tiles):
        ds_ref[j] = yd[:, j * LANES:(j + 1) * LANES]
    for g, (out_ref, (_, d)) in enumerate(zip((od1_ref, od4_ref, od16_ref), DIL_PATTERNS)):
        for part in range(3):
            src = part * len(DIL_PATTERNS) + g
            for r in range(d):
                rows = ds_ref[src] if d == 1 else ds_ref[src, pl.ds(r, tm // d, stride=d), :]
                out_ref[0, r, :, part * LANES:(part + 1) * LANES] = rows.astype(BF16)


def _inproj(x, g, wm, wg, wd, wz, conv_w, conv_b, cos, sin, q_norm, k_norm, neg_mb, B, T, tm):
    n = x.shape[0]
    tps = T // tm
    hb = tm // SUBLANES
    row = lambda w: pl.BlockSpec((tm, w), lambda i: (i, 0))
    full = lambda a: pl.BlockSpec(a.shape, lambda i: (0, 0))
    pos = pl.BlockSpec((tm, LANES), lambda i: (i % tps, 0))
    gqa_specs = [pl.BlockSpec((1, GQA_KV_HEADS, 1, LANES, GQA_GROUP * tm), lambda i: (i // tps, 0, i % tps, 0, 0)),
                 pl.BlockSpec((1, GQA_KV_HEADS, tm, LANES), lambda i: (i // tps, 0, i % tps, 0)),
                 pl.BlockSpec((1, GQA_KV_HEADS, GQA_VT_ROWS, tm), lambda i: (i // tps, 0, 0, i % tps))]
    gqa_shapes = [jax.ShapeDtypeStruct((B, GQA_KV_HEADS, tps, LANES, GQA_GROUP * tm), BF16),
                  jax.ShapeDtypeStruct((B, GQA_KV_HEADS, T, LANES), BF16),
                  jax.ShapeDtypeStruct((B, GQA_KV_HEADS, GQA_VT_ROWS, T), BF16)]
    dil_spec = lambda d: pl.BlockSpec((1, d, tm // d, 3 * LANES), lambda i: (i // tps, 0, i % tps, 0))
    dil_shape = lambda d: jax.ShapeDtypeStruct((B, d, T // d, 3 * LANES), BF16)
    dils = [d for _, d in DIL_PATTERNS]
    halo_prev = pl.BlockSpec((SUBLANES, D_MODEL), lambda i: (jnp.maximum(i * hb - 1, 0), 0))
    halo_next = pl.BlockSpec((SUBLANES, D_MODEL), lambda i: (jnp.minimum((i + 1) * hb, n // SUBLANES - 1), 0))
    return pl.pallas_call(
        functools.partial(_inproj_kernel, tm=tm, tps=tps),
        grid=(n // tm,),
        in_specs=[row(D_MODEL), halo_prev, halo_next, full(g), full(wm), full(wg), full(wd), full(wz),
                  full(conv_w), full(conv_b), pos, pos, full(q_norm), full(k_norm), full(neg_mb)],
        out_specs=[row(3 * MLSTM_W), row(GATE_PAD)] + gqa_specs + [dil_spec(d) for d in dils] + [row(Z_W)],
        out_shape=[jax.ShapeDtypeStruct((n, 3 * MLSTM_W), BF16), jax.ShapeDtypeStruct((n, GATE_PAD), F32)]
        + gqa_shapes + [dil_shape(d) for d in dils] + [jax.ShapeDtypeStruct((n, Z_W), BF16)],
        scratch_shapes=[pltpu.VMEM((D1_W // LANES, tm, LANES), F32)],
        compiler_params=_params(("parallel",)),
        name="inproj",
    )(x, x, x, g, wm, wg, wd, wz, conv_w, conv_b, cos, sin, q_norm, k_norm, neg_mb)


MLSTM_EXP2_CAP = 126.0


def _mlstm_consts():
    L, W, H = MLSTM_CHUNK, MLSTM_W, MLSTM_HEADS
    r = np.arange(L)
    tri = np.stack([r[None, :] <= r[:, None], r[None, :] >= r[:, None]]).astype(np.float32)
    lane_head = np.arange(W) // HEAD_DIM
    spread = np.zeros((2, GATE_PAD, W), np.float32)
    neg = np.full((SUBLANES, GATE_PAD), NEG_INF, np.float32)
    for d in range(2):
        fo = (2 * d + 1) * H
        for h in range(H):
            spread[d, fo + h] = lane_head == h
        neg[d, fo:fo + H] = 0.0
    ones_stack = (np.repeat(np.arange(H), L)[:, None] == lane_head[None, :]).astype(np.float32)
    same_head = (lane_head[:, None] == lane_head[None, :]).astype(np.float32)
    return (jnp.asarray(tri, BF16), jnp.asarray(spread, BF16), jnp.asarray(neg), jnp.asarray(ones_stack, BF16),
            jnp.asarray(same_head, BF16), jnp.asarray(same_head))


def _mlstm_chunk(qkv_ref, gate_ref, out_ref, c_ref, n_ref, m_ref, gb_ref, tri_ref, spread_ref, neg_ref,
                 os_ref, shb_ref, shf_ref, carry, bi, d, row0, first, last):
    forward = d == 0
    st = 2 * bi + d
    L, W, H = MLSTM_CHUNK, MLSTM_W, MLSTM_HEADS
    rows = slice(row0, row0 + L)
    x = qkv_ref[bi, rows, :]
    q, k, v = x[:, :W], x[:, W:2 * W], x[:, 2 * W:]
    g = gate_ref[bi, rows, :] + gb_ref[...]
    fo = (2 * d + 1) * H
    s_heads = [lax.dot_general(q * os_ref[h * L:(h + 1) * L, :], k, (((1,), (1,)), ((), ())),
                               preferred_element_type=F32) for h in range(H)]
    cum = sum(jnp.dot(tri_ref[d], t, preferred_element_type=F32)
              for t in _split_terms(_log_sigmoid(g) * LOG2E, 3))
    yield
    u = pltpu.roll(g * LOG2E, H, 1) - cum
    u_t = u.T
    r_i = lax.broadcasted_iota(jnp.int32, (L, L), 0)
    c_i = lax.broadcasted_iota(jnp.int32, (L, L), 1)
    seen = (c_i <= r_i) if forward else (c_i >= r_i)
    gl = lax.broadcasted_iota(jnp.int32, (L, GATE_PAD), 1)
    m_state = m_ref[st][0:1, :] if first else carry[st, "m"]
    m_row = jnp.zeros((L, GATE_PAD), F32)
    w_blocks = []
    for h in range(H):
        col = fo + h
        bc = cum[:, col:col + 1]
        dmat = jnp.where(seen, bc + u_t[col:col + 1, :], NEG_INF)
        m_row_h = jnp.maximum(bc + m_state[:, col:col + 1], jnp.max(dmat, axis=1, keepdims=True))
        w_blocks.append((jnp.exp2(dmat - m_row_h) * s_heads[h]).astype(BF16))
        m_row = jnp.where(gl == col, m_row_h, m_row)

    neg = neg_ref[d:d + 1, :]
    tot_row = L - 1 if forward else 0
    b_tot = cum[tot_row:tot_row + 1, :]
    gv = b_tot + u
    m_new = jnp.maximum(b_tot + m_state, jnp.max(gv, axis=0, keepdims=True))
    carry[st, "m"] = jnp.where(neg == 0.0, m_new, 0.0)
    decay = jnp.exp2(b_tot + m_state - m_new + neg)
    stack = jnp.concatenate([jnp.exp2(cum + m_state - m_row + neg),
                             jnp.exp2(jnp.minimum(-m_row, MLSTM_EXP2_CAP) + neg),
                             jnp.exp2(gv - m_new + neg),
                             jnp.broadcast_to(decay, (SUBLANES, GATE_PAD))], axis=0)
    spread = sum(jnp.dot(t, spread_ref[d], preferred_element_type=F32) for t in _split_terms(stack, 2))
    w_inter, inv_floor, k_fac, decay_e = spread[:L], spread[L:2 * L], spread[2 * L:3 * L], spread[3 * L:3 * L + 1]
    yield
    if not first:
        yield
    c_state = c_ref[st] if first else carry[st, "c"]
    n_state = n_ref[st][0:1, :] if first else carry[st, "n"]
    q_c = jnp.dot(q, c_state.astype(BF16), preferred_element_type=F32)
    q_n = jnp.dot((q.astype(F32) * n_state).astype(BF16), shb_ref[...], preferred_element_type=F32)
    w_cat = jnp.concatenate(w_blocks, axis=1)
    v_stack = os_ref[...] * jnp.concatenate([v] * H, axis=0)
    num = w_inter * q_c + jnp.dot(w_cat, v_stack, preferred_element_type=F32)
    den = w_inter * q_n + jnp.dot(w_cat, os_ref[...], preferred_element_type=F32)
    out_ref[bi, rows, :] = num / jnp.maximum(jnp.abs(den), inv_floor)
    yield
    kw = k_fac.astype(BF16) * k
    kv = lax.dot_general(kw, v, (((0,), (0,)), ((), ())), preferred_element_type=F32)
    k_sum = jnp.dot(jnp.ones((2 * SUBLANES, L), BF16), kw, preferred_element_type=F32)[0:1, :]
    carry[st, "c"] = decay_e * c_state + jnp.where(shf_ref[...] != 0.0, kv, 0.0)
    carry[st, "n"] = decay_e * n_state + k_sum
    if last:
        c_ref[st] = carry[st, "c"]
        n_ref[st] = jnp.broadcast_to(carry[st, "n"], (SUBLANES, W))
        m_ref[st] = jnp.broadcast_to(carry[st, "m"], (SUBLANES, GATE_PAD))


def _mlstm_kernel(qf_ref, gf_ref, qb_ref, gbk_ref, gb_ref, tri_ref, spread_ref, neg_ref, os_ref, shb_ref, shf_ref,
                  hf_ref, hb_ref, c_ref, n_ref, m_ref, *, bb, cpb):
    @pl.when(pl.program_id(1) == 0)
    def _():
        c_ref[...] = jnp.zeros_like(c_ref)
        n_ref[...] = jnp.zeros_like(n_ref)
        m_ref[...] = jnp.zeros_like(m_ref)

    consts = (gb_ref, tri_ref, spread_ref, neg_ref, os_ref, shb_ref, shf_ref)
    state = (c_ref, n_ref, m_ref)
    carry = {}
    chains = []
    for pos in range(cpb):
        for bi in range(bb):
            edge = (pos == 0, pos == cpb - 1)
            chains.append(_mlstm_chunk(qf_ref, gf_ref, hf_ref, *state, *consts, carry, bi, 0,
                                       pos * MLSTM_CHUNK, *edge))
            chains.append(_mlstm_chunk(qb_ref, gbk_ref, hb_ref, *state, *consts, carry, bi, 1,
                                       (cpb - 1 - pos) * MLSTM_CHUNK, *edge))
    _run_interleaved(chains)


def _mlstm(qkv, gates, gate_bias):
    B, T, _ = qkv.shape
    L = MLSTM_CHUNK
    bb = 2 if B % 2 == 0 else 1
    cpb = 2 if bb == 1 and (T // L) % 2 == 0 else 1
    nc = T // (L * cpb)
    consts = (gate_bias,) + _mlstm_consts()
    full = lambda a: pl.BlockSpec(a.shape, lambda b, c: (0,) * a.ndim)
    chunk = lambda w, pos: pl.BlockSpec((bb, cpb * L, w), lambda b, c: (b, pos(c), 0))
    fwd = lambda c: c
    bwd = lambda c: nc - 1 - c
    return pl.pallas_call(
        functools.partial(_mlstm_kernel, bb=bb, cpb=cpb),
        grid=(B // bb, nc),
        in_specs=[chunk(3 * MLSTM_W, fwd), chunk(GATE_PAD, fwd), chunk(3 * MLSTM_W, bwd), chunk(GATE_PAD, bwd)]
        + [full(a) for a in consts],
        out_specs=[chunk(MLSTM_W, fwd), chunk(MLSTM_W, bwd)],
        out_shape=[jax.ShapeDtypeStruct((B, T, MLSTM_W), F32)] * 2,
        scratch_shapes=[pltpu.VMEM((2 * bb, MLSTM_W, MLSTM_W), F32),
                        pltpu.VMEM((2 * bb, SUBLANES, MLSTM_W), F32),
                        pltpu.VMEM((2 * bb, SUBLANES, GATE_PAD), F32)],
        compiler_params=_params(("parallel", "arbitrary")),
        name="mlstm",
    )(qkv, gates, qkv, gates, *consts)


GQA_SAFE_LOG2 = 56.0
GQA_BOUND_SLACK = 1.0 + 2.0 ** -6
GQA_SCALE = HEAD_DIM ** -0.5 * LOG2E
GQA_VT_ROWS = LANES


def _rope_tile(a, cos, sin):
    half = HEAD_DIM // 2
    lane = lax.broadcasted_iota(jnp.int32, a.shape, 1) % HEAD_DIM
    rot = jnp.where(lane < half, pltpu.roll(a, LANES - half, 1), pltpu.roll(a, half, 1))
    return a * cos + rot * sin


def _head_sumsq(a):
    w = a.shape[1]
    same_head = jnp.where(_head_id((w, w), 0) == _head_id((w, w), 1), 1.0, 0.0).astype(BF16)
    return sum(jnp.dot(t, same_head, preferred_element_type=F32) for t in _split_terms(a * a, 2))


def _head_rms(a, gain):
    return a * lax.rsqrt(_head_sumsq(a) * (1.0 / HEAD_DIM) + NORM_EPS) * gain


def _gqa_prep_tile(x, cos_ref, sin_ref, qn_ref, kn_ref, nmb_ref, qt_out, k_out, vt_out):
    cos = cos_ref[...]
    sin = sin_ref[...]
    q = _head_rms(x[:, :GQA_W], qn_ref[...])
    k = _head_rms(x[:, GQA_W:GQA_W + GQA_KV_W], kn_ref[...])
    v = x[:, GQA_W + GQA_KV_W:]
    lane = lax.broadcasted_iota(jnp.int32, (x.shape[0], LANES), 1)
    first = lane < HEAD_DIM
    bound_col = jnp.where(lane == HEAD_DIM, nmb_ref[...], 0.0)
    ones_col = jnp.where(lane == HEAD_DIM, 1.0, 0.0)
    q_t = []
    for j in range(GQA_W // LANES):
        pair = _rope_tile(q[:, j * LANES:(j + 1) * LANES], cos, sin) * GQA_SCALE
        q_t.append(jnp.where(first, pair, bound_col).T.astype(BF16))
        q_t.append(jnp.where(first, pltpu.roll(pair, HEAD_DIM, 1), bound_col).T.astype(BF16))
    for kk in range(GQA_KV_HEADS):
        qt_out[0, kk, 0] = jnp.concatenate(q_t[kk * GQA_GROUP:(kk + 1) * GQA_GROUP], axis=1)
    k_rot = _rope_tile(k, cos, sin)
    k_out[0, 0] = jnp.where(first, k_rot, ones_col).astype(BF16)
    k_out[0, 1] = jnp.where(first, pltpu.roll(k_rot, HEAD_DIM, 1), ones_col).astype(BF16)
    vt_out[0, 0] = jnp.where(first, v, 1.0).T[:GQA_VT_ROWS].astype(BF16)
    vt_out[0, 1] = jnp.where(first, pltpu.roll(v, HEAD_DIM, 1), 1.0).T[:GQA_VT_ROWS].astype(BF16)


def _gqa_bound(q_norm, k_norm):
    return (HEAD_DIM * GQA_SCALE * GQA_BOUND_SLACK) * jnp.max(jnp.abs(q_norm)) * jnp.max(jnp.abs(k_norm))


def _gqa_finish(o_ref, acc_t, kh, tq):
    o_t = acc_t[:HEAD_DIM, :] / acc_t[HEAD_DIM:HEAD_DIM + 1, :]
    o = jnp.concatenate([o_t, o_t], axis=0).T
    for kk in range(GQA_KV_HEADS):
        @pl.when(kh == kk)
        def _():
            for h in range(GQA_GROUP):
                lo = (kk * GQA_GROUP + h) * HEAD_DIM
                o_ref[0, :, lo:lo + HEAD_DIM] = o[h * tq:(h + 1) * tq, :HEAD_DIM].astype(o_ref.dtype)


def _gqa_fast_kernel(qt_ref, k_ref, vt_ref, o_ref, acc_ref, *, tq, tk, kc, nk):
    kh = pl.program_id(2)
    ki = pl.program_id(3)

    @pl.when(ki == 0)
    def _():
        acc_ref[...] = jnp.zeros_like(acc_ref)

    qt = qt_ref[0, 0, 0]
    scores = lambda c: jnp.dot(k_ref[0, 0, c * kc:(c + 1) * kc, :], qt, preferred_element_type=F32)
    n_chunks = tk // kc
    acc = None
    st_next = scores(0)
    for c in range(n_chunks):
        st = st_next
        if c + 1 < n_chunks:
            st_next = scores(c + 1)
        pt = jnp.exp2(st).astype(BF16)
        d = jnp.dot(vt_ref[0, 0, :, c * kc:(c + 1) * kc], pt, preferred_element_type=F32)
        acc = d if acc is None else acc + d
    acc_ref[...] += acc

    @pl.when(ki == nk - 1)
    def _():
        _gqa_finish(o_ref, acc_ref[...], kh, tq)


def _gqa_online_kernel(qt_ref, k_ref, vt_ref, o_ref, m_ref, acc_ref, *, tq, nk):
    kh = pl.program_id(2)
    ki = pl.program_id(3)

    @pl.when(ki == 0)
    def _():
        m_ref[...] = jnp.full_like(m_ref, NEG_INF)
        acc_ref[...] = jnp.zeros_like(acc_ref)

    st = jnp.dot(k_ref[0, 0][:, :HEAD_DIM], qt_ref[0, 0, 0][:HEAD_DIM, :],
                 preferred_element_type=F32)
    m_prev = m_ref[0:1, :]
    m_new = jnp.maximum(m_prev, jnp.max(st, axis=0, keepdims=True))
    alpha = jnp.exp2(m_prev - m_new)
    pt = jnp.exp2(st - m_new).astype(BF16)
    acc_ref[...] = alpha * acc_ref[...] + jnp.dot(vt_ref[0, 0], pt, preferred_element_type=F32)
    m_ref[...] = jnp.broadcast_to(m_new, m_ref.shape)

    @pl.when(ki == nk - 1)
    def _():
        _gqa_finish(o_ref, acc_ref[...], kh, tq)


def _gqa_attend(qt, k, vt, fast, tq):
    B, _, T, _ = k.shape
    tk = min(8192, T) if fast else 512
    nq, nk = T // tq, T // tk
    n_cols = GQA_GROUP * tq
    acc = pltpu.VMEM((GQA_VT_ROWS, n_cols), F32)
    if fast:
        body = functools.partial(_gqa_fast_kernel, tq=tq, tk=tk, kc=256, nk=nk)
        scratch = [acc]
    else:
        body = functools.partial(_gqa_online_kernel, tq=tq, nk=nk)
        scratch = [pltpu.VMEM((SUBLANES, n_cols), F32), acc]
    return pl.pallas_call(
        body,
        grid=(B, nq, GQA_KV_HEADS, nk),
        in_specs=[pl.BlockSpec((1, 1, 1, LANES, n_cols), lambda b, i, h, j: (b, h, i, 0, 0)),
                  pl.BlockSpec((1, 1, tk, LANES), lambda b, i, h, j: (b, h, j, 0)),
                  pl.BlockSpec((1, 1, GQA_VT_ROWS, tk), lambda b, i, h, j: (b, h, 0, j))],
        out_specs=pl.BlockSpec((1, tq, GQA_W), lambda b, i, h, j: (b, i, 0)),
        out_shape=jax.ShapeDtypeStruct((B, T, GQA_W), BF16),
        scratch_shapes=scratch,
        compiler_params=_params(("parallel", "arbitrary", "arbitrary", "arbitrary")),
        name="gqa_fast" if fast else "gqa_online",
    )(qt, k, vt)


def _gqa(qt, k, vt, mb, tq):
    return lax.cond(mb <= GQA_SAFE_LOG2,
                    functools.partial(_gqa_attend, fast=True, tq=tq),
                    functools.partial(_gqa_attend, fast=False, tq=tq), qt, k, vt)


DIL_UNITS = 8


def _dil_kernel(q_ref, k_ref, v_ref, o_ref, lse_ref, *, S, bq, kw, nb, rb, dilation, slopes):
    base = pl.program_id(2) * (nb * bq)
    row = lax.broadcasted_iota(jnp.int32, (bq, kw), 0)
    col = lax.broadcasted_iota(jnp.int32, (bq, kw), 1)
    hid = _head_id((bq, LANES), 1)

    def unit(rr, n):
        p0 = base + n * bq
        if kw == S:
            start = 0
            kwin, vwin = k_ref[0, rr], v_ref[0, rr]
        else:
            start = pl.multiple_of(jnp.clip(p0 - DIL_SIDE, 0, S - kw), DIL_SIDE)
            kwin = k_ref[0, rr, pl.ds(start, kw), :]
            vwin = v_ref[0, rr, pl.ds(start, kw), :]
        q = q_ref[0, rr, n * bq:(n + 1) * bq, :]
        scores = [lax.dot_general(jnp.where(hid == hh, q, jnp.zeros_like(q)), kwin, (((1,), (1,)), ((), ())),
                                  preferred_element_type=F32) for hh in range(DIL_HEADS_PER_GROUP)]
        yield
        rel = jnp.abs(col - row + (start - p0))
        valid = rel <= DIL_SIDE
        dist = (rel * dilation).astype(F32)
        outs, lses = [], []
        for hh in range(DIL_HEADS_PER_GROUP):
            s = jnp.where(valid, scores[hh] * HEAD_DIM ** -0.5 - slopes[hh] * dist, NEG_INF)
            m = jnp.max(s, axis=-1, keepdims=True)
            p = jnp.exp(s - m)
            l = jnp.sum(p, axis=-1, keepdims=True)
            outs.append(jnp.dot(p.astype(BF16), vwin, preferred_element_type=F32) / l)
            lses.append(m + jnp.log(l))
        o_ref[0, rr, n * bq:(n + 1) * bq, :] = jnp.where(hid == 0, outs[0], outs[1]).astype(o_ref.dtype)
        lse_ref[0, rr, n * bq:(n + 1) * bq, :] = jnp.where(hid == 0, lses[0], lses[1])

    _run_interleaved([unit(rr, n) for rr in range(rb) for n in range(nb)])


def _dilated(qkv, dilation, slopes):
    B, d, S, _ = qkv.shape
    bq = min(128, S)
    kw = min(bq + 2 * DIL_SIDE, S)
    nb = min(DIL_UNITS, S // bq)
    rb = min(d, DIL_UNITS // nb)
    seq = lambda part: pl.BlockSpec((1, rb, S, LANES), lambda b, r, i: (b, r, 0, part))
    blk = lambda part: pl.BlockSpec((1, rb, nb * bq, LANES), lambda b, r, i: (b, r, i, part))
    return pl.pallas_call(
        functools.partial(_dil_kernel, S=S, bq=bq, kw=kw, nb=nb, rb=rb, dilation=dilation, slopes=slopes),
        grid=(B, d // rb, S // (nb * bq)),
        in_specs=[blk(0), seq(1), seq(2)],
        out_specs=[blk(0), blk(0)],
        out_shape=[jax.ShapeDtypeStruct((B, d, S, LANES), BF16), jax.ShapeDtypeStruct((B, d, S, LANES), F32)],
        compiler_params=_params(("parallel", "parallel", "arbitrary")),
        name=f"dilated_{dilation}",
    )(qkv, qkv, qkv)


def _outproj_kernel(x_ref, hf_ref, hb_ref, z_ref, b_ref, o0_ref, o1_ref, o2_ref, l0_ref, l1_ref, l2_ref,
                    gain_ref, w_ref, gpost_ref, out_ref, nat_ref, *, tm):
    z = z_ref[...].astype(F32)
    mo = z[:, :MLSTM_W]
    mz = z[:, MLSTM_W:2 * MLSTM_W]
    gz = z[:, 2 * MLSTM_W:2 * MLSTM_W + GQA_W]
    dz = z[:, 2 * MLSTM_W + GQA_W:]

    h = _sigmoid(mo) * (hf_ref[...] + hb_ref[...])
    a = h * lax.rsqrt(_head_sumsq(h) * (1.0 / HEAD_DIM) + NORM_EPS) * gain_ref[...] * _silu(mz)

    b = b_ref[...].astype(F32) * _silu(gz)

    n_groups = len(DIL_PATTERNS)
    for g, (o_ref, l_ref, (_, d)) in enumerate(zip((o0_ref, o1_ref, o2_ref), (l0_ref, l1_ref, l2_ref), DIL_PATTERNS)):
        for r in range(d):
            dst = slice(None) if d == 1 else pl.ds(r, tm // d, stride=d)
            nat_ref[g, dst, :] = o_ref[0, r].astype(F32)
            nat_ref[n_groups + g, dst, :] = l_ref[0, r]
    outs = [nat_ref[g] for g in range(n_groups)]
    lses = [nat_ref[n_groups + g] for g in range(n_groups)]
    mx = jnp.maximum(jnp.maximum(lses[0], lses[1]), lses[2])
    es = [jnp.exp(l - mx) for l in lses]
    inv = 1.0 / (es[0] + es[1] + es[2])
    c = jnp.concatenate([outs[g] * (es[g] * inv) for g in range(3)], axis=1) * _silu(dz)

    w = w_ref[...]
    y = jnp.dot(a.astype(BF16), w[:MLSTM_W], preferred_element_type=F32)
    y += jnp.dot(b.astype(BF16), w[MLSTM_W:MLSTM_W + GQA_W], preferred_element_type=F32)
    y += jnp.dot(c.astype(BF16), w[MLSTM_W + GQA_W:], preferred_element_type=F32)
    ms = jnp.mean(y * y, axis=-1, keepdims=True)
    out_ref[...] = x_ref[...] + y * lax.rsqrt(ms + NORM_EPS) * gpost_ref[...]


def _outproj(x, hf, hb, z, b, dil_o, dil_l, gain, w, gpost, T, tm=256):
    n = x.shape[0]
    tps = T // tm
    row = lambda a: pl.BlockSpec((tm, a.shape[1]), lambda i: (i, 0))
    full = lambda a: pl.BlockSpec(a.shape, lambda i: (0, 0))
    cls = lambda a: pl.BlockSpec((1, a.shape[1], tm // a.shape[1], LANES), lambda i: (i // tps, 0, i % tps, 0))
    rows = [x, hf, hb, z, b]
    dil = list(dil_o) + list(dil_l)
    consts = [gain, w, gpost]
    return pl.pallas_call(
        functools.partial(_outproj_kernel, tm=tm),
        grid=(n // tm,),
        in_specs=[row(a) for a in rows] + [cls(a) for a in dil] + [full(a) for a in consts],
        out_specs=pl.BlockSpec((tm, D_MODEL), lambda i: (i, 0)),
        out_shape=jax.ShapeDtypeStruct((n, D_MODEL), F32),
        scratch_shapes=[pltpu.VMEM((2 * len(DIL_PATTERNS), tm, LANES), F32)],
        compiler_params=_params(("parallel",)),
        name="outproj",
    )(*rows, *dil, *consts)


def _rope_tables(T):
    rows = T // GRID_W
    r, c = jnp.meshgrid(jnp.arange(rows), jnp.arange(GRID_W), indexing="ij")
    r = r.reshape(-1).astype(F32)
    c = c.reshape(-1).astype(F32)
    axis_dim = HEAD_DIM // 2
    inv = ROPE_THETA ** (-jnp.arange(0, axis_dim, 2, dtype=F32) / axis_dim)
    ang = jnp.concatenate([r[:, None] * inv, c[:, None] * inv], axis=-1)
    cos, sin = jnp.cos(ang), jnp.sin(ang)
    cos_t = jnp.tile(jnp.concatenate([cos, cos], axis=-1), (1, LANES // HEAD_DIM))
    sin_t = jnp.tile(jnp.concatenate([-sin, sin], axis=-1), (1, LANES // HEAD_DIM))
    return cos_t, sin_t


def _split_w_in(w):
    cuts = np.cumsum((0,) + IN_SPLITS)
    mq, mk, mv, mo, mg, mz, gq, gk, gv, gz, dq, dk, dv, dz = [w[:, cuts[i]:cuts[i + 1]] for i in range(14)]
    pad = jnp.zeros((w.shape[0], GATE_PAD - mg.shape[1]), w.dtype)
    cat = lambda *xs: jnp.concatenate(xs, axis=1).astype(BF16)
    return cat(mq, mk, mv, mg, pad), cat(gq, gk, gv), cat(dq, dk, dv), cat(mo, mz, gz, dz)


def _alibi_slopes():
    return [float(np.exp2(np.float32(-8.0) * np.float32(i) / np.float32(DIL_HEADS)))
            for i in range(1, DIL_HEADS + 1)]


def _layer(x, B, T, p, cos, sin, slopes):
    n = B * T
    mb = _gqa_bound(p["q_norm"], p["k_norm"])
    m_qkv, m_gates, qt, k, vt, *dil_qkv, z = _inproj(
        x, p["norm_pre"], *p["w_in"], p["conv_w"], p["conv_b"], cos, sin, p["q_norm"], p["k_norm"],
        jnp.full((1, LANES), -mb, F32), B, T, TOKEN_TILE)
    hf, hb = _mlstm(m_qkv.reshape(B, T, 3 * MLSTM_W), m_gates.reshape(B, T, GATE_PAD), p["gate_bias"])
    b = _gqa(qt, k, vt, mb, TOKEN_TILE)

    dil_o, dil_l = [], []
    for g, (_, dilation) in enumerate(DIL_PATTERNS):
        o_g, l_g = _dilated(dil_qkv[g], dilation, slopes[g * DIL_HEADS_PER_GROUP:(g + 1) * DIL_HEADS_PER_GROUP])
        dil_o.append(o_g)
        dil_l.append(l_g)

    return _outproj(x, hf.reshape(n, MLSTM_W), hb.reshape(n, MLSTM_W), z, b.reshape(n, GQA_W),
                    dil_o, dil_l, p["out_gain"], p["w_out"], p["norm_post"], T)


def _trunk(x, layers):
    B, T, _ = x.shape
    cos, sin = _rope_tables(T)
    slopes = _alibi_slopes()
    x = x.reshape(B * T, D_MODEL)
    for p in layers:
        x = _layer(x, B, T, p, cos, sin, slopes)
    return x.reshape(B, T, D_MODEL)


def kernel(x_prompt, x_sample, norm_pre, w_in, mlstm_gate_bias, mlstm_conv_w, mlstm_conv_b, mlstm_out_gain,
           gqa_q_norm, gqa_k_norm, w_out, norm_post):
    depth = w_in.shape[0]
    row = lambda a: a.reshape(1, -1).astype(F32)
    layers = []
    for l in range(depth):
        gate_bias = jnp.concatenate(
            [mlstm_gate_bias[l].astype(F32), jnp.zeros((GATE_PAD - 4 * MLSTM_HEADS,), F32)]).reshape(1, GATE_PAD)
        layers.append(dict(
            norm_pre=row(norm_pre[l]),
            w_in=_split_w_in(w_in[l]),
            gate_bias=gate_bias,
            conv_w=mlstm_conv_w[l].astype(F32),
            conv_b=row(mlstm_conv_b[l]),
            out_gain=row(mlstm_out_gain[l]),
            q_norm=row(jnp.tile(gqa_q_norm[l], GQA_Q_HEADS)),
            k_norm=row(jnp.tile(gqa_k_norm[l], GQA_KV_HEADS)),
            w_out=w_out[l].astype(BF16),
            norm_post=row(norm_post[l]),
        ))
    return (_trunk(x_prompt, layers), _trunk(x_sample, layers))
```

```python
import functools
import math

import numpy as np
import jax
import jax.numpy as jnp
from jax import lax
from jax.experimental import pallas as pl
from jax.experimental.pallas import tpu as pltpu

F32 = jnp.float32
BF16 = jnp.bfloat16

D_MODEL = 1024
HEAD_DIM = 64
NORM_EPS = 1e-6
GRID_W = 64
ROPE_THETA = 10000.0
MLSTM_HEADS = 4
MLSTM_W = MLSTM_HEADS * HEAD_DIM
MLSTM_CHUNK = 128
GQA_Q_HEADS = 6
GQA_KV_HEADS = 2
GQA_GROUP = GQA_Q_HEADS // GQA_KV_HEADS
GQA_W = GQA_Q_HEADS * HEAD_DIM
GQA_KV_W = GQA_KV_HEADS * HEAD_DIM
DIL_PATTERNS = ((128, 1), (512, 4), (2048, 16))
DIL_HEADS_PER_GROUP = 2
DIL_HEADS = DIL_HEADS_PER_GROUP * len(DIL_PATTERNS)
DIL_W = DIL_HEADS * HEAD_DIM
DIL_SIDE = 64
IN_SPLITS = (MLSTM_W, MLSTM_W, MLSTM_W, MLSTM_W, 4 * MLSTM_HEADS, MLSTM_W,
             GQA_W, GQA_KV_W, GQA_KV_W, GQA_W, DIL_W, DIL_W, DIL_W, DIL_W)

LANES = 128
SUBLANES = 8
V7X_VMEM_BYTES = 64 * 1024 * 1024
GATE_PAD = LANES
M1_W = 3 * MLSTM_W + GATE_PAD
G1_W = GQA_W + 2 * GQA_KV_W
D1_W = 3 * DIL_W
Z_W = 2 * MLSTM_W + GQA_W + DIL_W
TOKEN_TILE = 512
VMEM_LIMIT = 3 * V7X_VMEM_BYTES // 4
LOG2E = math.log2(math.e)
NEG_INF = float("-inf")


def _sigmoid(x):
    return 1.0 / (1.0 + jnp.exp(-x))


def _silu(x):
    return x * _sigmoid(x)


def _log_sigmoid(x):
    return jnp.minimum(x, 0.0) - jnp.log(1.0 + jnp.exp(-jnp.abs(x)))


def _head_id(shape, axis):
    return lax.broadcasted_iota(jnp.int32, shape, axis) // HEAD_DIM


def _run_interleaved(stage_generators):
    live = list(stage_generators)
    while live:
        live = [g for g in live if next(g, StopIteration) is not StopIteration]


def _params(sem):
    return pltpu.CompilerParams(dimension_semantics=sem, vmem_limit_bytes=VMEM_LIMIT)


def _split_terms(a, terms):
    out = []
    for t in range(terms):
        hi = a.astype(BF16)
        out.append(hi)
        if t + 1 < terms:
            a = a - hi.astype(F32)
    return out


def _inproj_kernel(x_ref, xp_ref, xn_ref, g_ref, w_ref, cw_ref, cb_ref,
                   cos_ref, sin_ref, qn_ref, kn_ref, nmb_ref,
                   oqkv_ref, ogate_ref, oqt_ref, ok_ref, ovt_ref, od1_ref, od4_ref, od16_ref, oz_ref, ds_ref,
                   *, tm, tps):
    def normed(x):
        ms = jnp.mean(x * x, axis=-1, keepdims=True)
        return (x * lax.rsqrt(ms + NORM_EPS) * g_ref[...]).astype(BF16)

    c0, c1, c2 = M1_W, M1_W + G1_W, M1_W + G1_W + D1_W
    wm_ref, wg_ref, wd_ref, wz_ref = (w_ref.at[:, :c0], w_ref.at[:, c0:c1], w_ref.at[:, c1:c2], w_ref.at[:, c2:])
    h = normed(x_ref[...])
    halo = normed(jnp.concatenate([xp_ref[...], xn_ref[...]], axis=0))
    ymh = jnp.dot(jnp.concatenate([h, halo], axis=0), wm_ref[...], preferred_element_type=F32)
    ym = ymh[:tm]
    qkw = 2 * MLSTM_W
    yh = ymh[tm:, :qkw]
    t = pl.program_id(0) % tps
    prev_row = yh[SUBLANES - 1:SUBLANES, :] * jnp.where(t > 0, 1.0, 0.0).astype(F32)
    next_row = yh[SUBLANES:SUBLANES + 1, :] * jnp.where(t < tps - 1, 1.0, 0.0).astype(F32)
    qk = ym[:, :qkw]
    rid = lax.broadcasted_iota(jnp.int32, (tm, qkw), 0)
    qk_m1 = jnp.where(rid == 0, prev_row, pltpu.roll(qk, 1, 0))
    qk_p1 = jnp.where(rid == tm - 1, next_row, pltpu.roll(qk, tm - 1, 0))
    cw = cw_ref[...]
    act = _silu(qk_m1 * cw[0:1, :] + qk * cw[1:2, :] + qk_p1 * cw[2:3, :] + cb_ref[...])
    oqkv_ref[:, :MLSTM_W] = act[:, :MLSTM_W].astype(BF16)
    oqkv_ref[:, MLSTM_W:qkw] = (act[:, MLSTM_W:] * HEAD_DIM ** -0.5).astype(BF16)
    oqkv_ref[:, qkw:] = ym[:, qkw:3 * MLSTM_W].astype(BF16)
    ogate_ref[...] = ym[:, 3 * MLSTM_W:]
    _gqa_prep_tile(jnp.dot(h, wg_ref[...], preferred_element_type=F32), cos_ref, sin_ref, qn_ref, kn_ref, nmb_ref,
                   oqt_ref, ok_ref, ovt_ref)
    oz_ref[...] = jnp.dot(h, wz_ref[...], preferred_element_type=F32).astype(BF16)
    yd = jnp.dot(h, wd_ref[...], preferred_element_type=F32)
    n_tiles = D1_W // LANES
    for j in range(n_tiles):
        ds_ref[j] = yd[:, j * LANES:(j + 1) * LANES]
    for g, (out_ref, (_, d)) in enumerate(zip((od1_ref, od4_ref, od16_ref), DIL_PATTERNS)):
        for part in range(3):
            src = part * len(DIL_PATTERNS) + g
            for r in range(d):
                rows = ds_ref[src] if d == 1 else ds_ref[src, pl.ds(r, tm // d, stride=d), :]
                out_ref[0, r, :, part * LANES:(part + 1) * LANES] = rows.astype(BF16)


def _inproj(x, g, w, conv_w, conv_b, cos, sin, q_norm, k_norm, neg_mb, B, T, tm):
    n = x.shape[0]
    tps = T // tm
    hb = tm // SUBLANES
    row = lambda w: pl.BlockSpec((tm, w), lambda i: (i, 0))
    full = lambda a: pl.BlockSpec(a.shape, lambda i: (0, 0))
    pos = pl.BlockSpec((tm, LANES), lambda i: (i % tps, 0))
    gqa_specs = [pl.BlockSpec((1, GQA_KV_HEADS, 1, LANES, GQA_GROUP * tm), lambda i: (i // tps, 0, i % tps, 0, 0)),
                 pl.BlockSpec((1, GQA_KV_HEADS, tm, LANES), lambda i: (i // tps, 0, i % tps, 0)),
                 pl.BlockSpec((1, GQA_KV_HEADS, GQA_VT_ROWS, tm), lambda i: (i // tps, 0, 0, i % tps))]
    gqa_shapes = [jax.ShapeDtypeStruct((B, GQA_KV_HEADS, tps, LANES, GQA_GROUP * tm), BF16),
                  jax.ShapeDtypeStruct((B, GQA_KV_HEADS, T, LANES), BF16),
                  jax.ShapeDtypeStruct((B, GQA_KV_HEADS, GQA_VT_ROWS, T), BF16)]
    dil_spec = lambda d: pl.BlockSpec((1, d, tm // d, 3 * LANES), lambda i: (i // tps, 0, i % tps, 0))
    dil_shape = lambda d: jax.ShapeDtypeStruct((B, d, T // d, 3 * LANES), BF16)
    dils = [d for _, d in DIL_PATTERNS]
    halo_prev = pl.BlockSpec((SUBLANES, D_MODEL), lambda i: (jnp.maximum(i * hb - 1, 0), 0))
    halo_next = pl.BlockSpec((SUBLANES, D_MODEL), lambda i: (jnp.minimum((i + 1) * hb, n // SUBLANES - 1), 0))
    return pl.pallas_call(
        functools.partial(_inproj_kernel, tm=tm, tps=tps),
        grid=(n // tm,),
        in_specs=[row(D_MODEL), halo_prev, halo_next, full(g), full(w),
                  full(conv_w), full(conv_b), pos, pos, full(q_norm), full(k_norm), full(neg_mb)],
        out_specs=[row(3 * MLSTM_W), row(GATE_PAD)] + gqa_specs + [dil_spec(d) for d in dils] + [row(Z_W)],
        out_shape=[jax.ShapeDtypeStruct((n, 3 * MLSTM_W), BF16), jax.ShapeDtypeStruct((n, GATE_PAD), F32)]
        + gqa_shapes + [dil_shape(d) for d in dils] + [jax.ShapeDtypeStruct((n, Z_W), BF16)],
        scratch_shapes=[pltpu.VMEM((D1_W // LANES, tm, LANES), F32)],
        compiler_params=_params(("parallel",)),
        name="inproj",
    )(x, x, x, g, w, conv_w, conv_b, cos, sin, q_norm, k_norm, neg_mb)


MLSTM_EXP2_CAP = 126.0


def _mlstm_consts():
    L, W, H = MLSTM_CHUNK, MLSTM_W, MLSTM_HEADS
    r = np.arange(L)
    tri = np.stack([r[None, :] <= r[:, None], r[None, :] >= r[:, None]]).astype(np.float32)
    lane_head = np.arange(W) // HEAD_DIM
    spread = np.zeros((2, GATE_PAD, W), np.float32)
    neg = np.full((SUBLANES, GATE_PAD), NEG_INF, np.float32)
    for d in range(2):
        fo = (2 * d + 1) * H
        for h in range(H):
            spread[d, fo + h] = lane_head == h
        neg[d, fo:fo + H] = 0.0
    ones_stack = (np.repeat(np.arange(H), L)[:, None] == lane_head[None, :]).astype(np.float32)
    same_head = (lane_head[:, None] == lane_head[None, :]).astype(np.float32)
    return (jnp.asarray(tri, BF16), jnp.asarray(spread, BF16), jnp.asarray(neg), jnp.asarray(ones_stack, BF16),
            jnp.asarray(same_head, BF16), jnp.asarray(same_head))


def _mlstm_chunk(qkv_ref, gate_ref, out_ref, c_ref, n_ref, m_ref, gb_ref, tri_ref, spread_ref, neg_ref,
                 os_ref, shb_ref, shf_ref, carry, bi, d, row0, first, last):
    forward = d == 0
    st = 2 * bi + d
    L, W, H = MLSTM_CHUNK, MLSTM_W, MLSTM_HEADS
    rows = slice(row0, row0 + L)
    x = qkv_ref[bi, rows, :]
    q, k, v = x[:, :W], x[:, W:2 * W], x[:, 2 * W:]
    g = gate_ref[bi, rows, :] + gb_ref[...]
    fo = (2 * d + 1) * H
    s_heads = [lax.dot_general(q * os_ref[h * L:(h + 1) * L, :], k, (((1,), (1,)), ((), ())),
                               preferred_element_type=F32) for h in range(H)]
    cum = sum(jnp.dot(tri_ref[d], t, preferred_element_type=F32)
              for t in _split_terms(_log_sigmoid(g) * LOG2E, 3))

    def read_state(c_state, n_state):
        q_n = jnp.dot((q.astype(F32) * n_state).astype(BF16), shb_ref[...], preferred_element_type=F32)
        return c_state, n_state, jnp.dot(q, c_state.astype(BF16), preferred_element_type=F32), q_n

    early = first and last
    if early:
        c_state, n_state, q_c, q_n = read_state(c_ref[st], n_ref[st][0:1, :])
    yield
    u = pltpu.roll(g * LOG2E, H, 1) - cum
    u_t = u.T
    r_i = lax.broadcasted_iota(jnp.int32, (L, L), 0)
    c_i = lax.broadcasted_iota(jnp.int32, (L, L), 1)
    seen = (c_i <= r_i) if forward else (c_i >= r_i)
    gl = lax.broadcasted_iota(jnp.int32, (L, GATE_PAD), 1)
    m_state = m_ref[st][0:1, :] if first else carry[st, "m"]
    m_row = jnp.zeros((L, GATE_PAD), F32)
    w_blocks = []
    for h in range(H):
        col = fo + h
        bc = cum[:, col:col + 1]
        dmat = jnp.where(seen, bc + u_t[col:col + 1, :], NEG_INF)
        m_row_h = jnp.maximum(bc + m_state[:, col:col + 1], jnp.max(dmat, axis=1, keepdims=True))
        w_blocks.append((jnp.exp2(dmat - m_row_h) * s_heads[h]).astype(BF16))
        m_row = jnp.where(gl == col, m_row_h, m_row)

    neg = neg_ref[d:d + 1, :]
    tot_row = L - 1 if forward else 0
    b_tot = cum[tot_row:tot_row + 1, :]
    gv = b_tot + u
    m_new = jnp.maximum(b_tot + m_state, jnp.max(gv, axis=0, keepdims=True))
    carry[st, "m"] = jnp.where(neg == 0.0, m_new, 0.0)
    decay = jnp.exp2(b_tot + m_state - m_new + neg)
    stack = jnp.concatenate([jnp.exp2(cum + m_state - m_row + neg),
                             jnp.exp2(jnp.minimum(-m_row, MLSTM_EXP2_CAP) + neg),
                             jnp.exp2(gv - m_new + neg),
                             jnp.broadcast_to(decay, (SUBLANES, GATE_PAD))], axis=0)
    spread = sum(jnp.dot(t, spread_ref[d], preferred_element_type=F32) for t in _split_terms(stack, 2))
    w_inter, inv_floor, k_fac, decay_e = spread[:L], spread[L:2 * L], spread[2 * L:3 * L], spread[3 * L:3 * L + 1]
    yield
    if not first:
        yield
        c_state, n_state, q_c, q_n = read_state(carry[st, "c"], carry[st, "n"])
    elif not early:
        c_state, n_state, q_c, q_n = read_state(c_ref[st], n_ref[st][0:1, :])
    w_cat = jnp.concatenate(w_blocks, axis=1)
    v_stack = os_ref[...] * jnp.concatenate([v] * H, axis=0)
    num = w_inter * q_c + jnp.dot(w_cat, v_stack, preferred_element_type=F32)
    den = w_inter * q_n + jnp.dot(w_cat, os_ref[...], preferred_element_type=F32)
    out_ref[bi, rows, :] = num / jnp.maximum(jnp.abs(den), inv_floor)
    yield
    kw = k_fac.astype(BF16) * k
    kv = lax.dot_general(kw, v, (((0,), (0,)), ((), ())), preferred_element_type=F32)
    k_sum = jnp.dot(jnp.ones((2 * SUBLANES, L), BF16), kw, preferred_element_type=F32)[0:1, :]
    carry[st, "c"] = decay_e * c_state + jnp.where(shf_ref[...] != 0.0, kv, 0.0)
    carry[st, "n"] = decay_e * n_state + k_sum
    if last:
        c_ref[st] = carry[st, "c"]
        n_ref[st] = jnp.broadcast_to(carry[st, "n"], (SUBLANES, W))
        m_ref[st] = jnp.broadcast_to(carry[st, "m"], (SUBLANES, GATE_PAD))


def _mlstm_kernel(qf_ref, gf_ref, qb_ref, gbk_ref, gb_ref, tri_ref, spread_ref, neg_ref, os_ref, shb_ref, shf_ref,
                  hf_ref, hb_ref, c_ref, n_ref, m_ref, *, bb, cpb):
    @pl.when(pl.program_id(1) == 0)
    def _():
        c_ref[...] = jnp.zeros_like(c_ref)
        n_ref[...] = jnp.zeros_like(n_ref)
        m_ref[...] = jnp.zeros_like(m_ref)

    consts = (gb_ref, tri_ref, spread_ref, neg_ref, os_ref, shb_ref, shf_ref)
    state = (c_ref, n_ref, m_ref)
    carry = {}
    chains = []
    for pos in range(cpb):
        for bi in range(bb):
            edge = (pos == 0, pos == cpb - 1)
            chains.append(_mlstm_chunk(qf_ref, gf_ref, hf_ref, *state, *consts, carry, bi, 0,
                                       pos * MLSTM_CHUNK, *edge))
            chains.append(_mlstm_chunk(qb_ref, gbk_ref, hb_ref, *state, *consts, carry, bi, 1,
                                       (cpb - 1 - pos) * MLSTM_CHUNK, *edge))
    _run_interleaved(chains)


def _mlstm(qkv, gates, gate_bias):
    B, T, _ = qkv.shape
    L = MLSTM_CHUNK
    bb = 2 if B % 2 == 0 else 1
    cpb = 2 if bb == 1 and (T // L) % 2 == 0 else 1
    nc = T // (L * cpb)
    consts = (gate_bias,) + _mlstm_consts()
    full = lambda a: pl.BlockSpec(a.shape, lambda b, c: (0,) * a.ndim)
    chunk = lambda w, pos: pl.BlockSpec((bb, cpb * L, w), lambda b, c: (b, pos(c), 0))
    fwd = lambda c: c
    bwd = lambda c: nc - 1 - c
    return pl.pallas_call(
        functools.partial(_mlstm_kernel, bb=bb, cpb=cpb),
        grid=(B // bb, nc),
        in_specs=[chunk(3 * MLSTM_W, fwd), chunk(GATE_PAD, fwd), chunk(3 * MLSTM_W, bwd), chunk(GATE_PAD, bwd)]
        + [full(a) for a in consts],
        out_specs=[chunk(MLSTM_W, fwd), chunk(MLSTM_W, bwd)],
        out_shape=[jax.ShapeDtypeStruct((B, T, MLSTM_W), F32)] * 2,
        scratch_shapes=[pltpu.VMEM((2 * bb, MLSTM_W, MLSTM_W), F32),
                        pltpu.VMEM((2 * bb, SUBLANES, MLSTM_W), F32),
                        pltpu.VMEM((2 * bb, SUBLANES, GATE_PAD), F32)],
        compiler_params=_params(("parallel", "arbitrary")),
        name="mlstm",
    )(qkv, gates, qkv, gates, *consts)


GQA_SAFE_LOG2 = 56.0
GQA_BOUND_SLACK = 1.0 + 2.0 ** -6
GQA_SCALE = HEAD_DIM ** -0.5 * LOG2E
GQA_VT_ROWS = LANES


def _rope_tile(a, cos, sin):
    half = HEAD_DIM // 2
    lane = lax.broadcasted_iota(jnp.int32, a.shape, 1) % HEAD_DIM
    rot = jnp.where(lane < half, pltpu.roll(a, LANES - half, 1), pltpu.roll(a, half, 1))
    return a * cos + rot * sin


def _head_sumsq(a):
    w = a.shape[1]
    same_head = jnp.where(_head_id((w, w), 0) == _head_id((w, w), 1), 1.0, 0.0).astype(BF16)
    return sum(jnp.dot(t, same_head, preferred_element_type=F32) for t in _split_terms(a * a, 2))


def _head_rms(a, gain):
    return a * lax.rsqrt(_head_sumsq(a) * (1.0 / HEAD_DIM) + NORM_EPS) * gain


def _gqa_prep_tile(x, cos_ref, sin_ref, qn_ref, kn_ref, nmb_ref, qt_out, k_out, vt_out):
    cos = cos_ref[...]
    sin = sin_ref[...]
    q = _head_rms(x[:, :GQA_W], qn_ref[...])
    k = _head_rms(x[:, GQA_W:GQA_W + GQA_KV_W], kn_ref[...])
    v = x[:, GQA_W + GQA_KV_W:]
    lane = lax.broadcasted_iota(jnp.int32, (x.shape[0], LANES), 1)
    first = lane < HEAD_DIM
    bound_col = jnp.where(lane == HEAD_DIM, nmb_ref[...], 0.0)
    ones_col = jnp.where(lane == HEAD_DIM, 1.0, 0.0)
    q_t = []
    for j in range(GQA_W // LANES):
        pair = _rope_tile(q[:, j * LANES:(j + 1) * LANES], cos, sin) * GQA_SCALE
        q_t.append(jnp.where(first, pair, bound_col).T.astype(BF16))
        q_t.append(jnp.where(first, pltpu.roll(pair, HEAD_DIM, 1), bound_col).T.astype(BF16))
    for kk in range(GQA_KV_HEADS):
        qt_out[0, kk, 0] = jnp.concatenate(q_t[kk * GQA_GROUP:(kk + 1) * GQA_GROUP], axis=1)
    k_rot = _rope_tile(k, cos, sin)
    k_out[0, 0] = jnp.where(first, k_rot, ones_col).astype(BF16)
    k_out[0, 1] = jnp.where(first, pltpu.roll(k_rot, HEAD_DIM, 1), ones_col).astype(BF16)
    vt_out[0, 0] = jnp.where(first, v, 1.0).T[:GQA_VT_ROWS].astype(BF16)
    vt_out[0, 1] = jnp.where(first, pltpu.roll(v, HEAD_DIM, 1), 1.0).T[:GQA_VT_ROWS].astype(BF16)


def _gqa_bound(q_norm, k_norm):
    return (HEAD_DIM * GQA_SCALE * GQA_BOUND_SLACK) * jnp.max(jnp.abs(q_norm)) * jnp.max(jnp.abs(k_norm))


def _gqa_finish(o_ref, acc_t, kh, tq):
    o_t = acc_t[:HEAD_DIM, :] / acc_t[HEAD_DIM:HEAD_DIM + 1, :]
    o = jnp.concatenate([o_t, o_t], axis=0).T
    for kk in range(GQA_KV_HEADS):
        @pl.when(kh == kk)
        def _():
            for h in range(GQA_GROUP):
                lo = (kk * GQA_GROUP + h) * HEAD_DIM
                o_ref[0, :, lo:lo + HEAD_DIM] = o[h * tq:(h + 1) * tq, :HEAD_DIM].astype(o_ref.dtype)


def _gqa_fast_kernel(qt_ref, k_ref, vt_ref, o_ref, acc_ref, *, tq, tk, kc, nk):
    kh = pl.program_id(2)
    ki = pl.program_id(3)

    @pl.when(ki == 0)
    def _():
        acc_ref[...] = jnp.zeros_like(acc_ref)

    qt = qt_ref[0, 0, 0]
    scores = lambda c: jnp.dot(k_ref[0, 0, c * kc:(c + 1) * kc, :], qt, preferred_element_type=F32)
    n_chunks = tk // kc
    acc = None
    st_next = scores(0)
    for c in range(n_chunks):
        st = st_next
        if c + 1 < n_chunks:
            st_next = scores(c + 1)
        pt = jnp.exp2(st).astype(BF16)
        d = jnp.dot(vt_ref[0, 0, :, c * kc:(c + 1) * kc], pt, preferred_element_type=F32)
        acc = d if acc is None else acc + d
    acc_ref[...] += acc

    @pl.when(ki == nk - 1)
    def _():
        _gqa_finish(o_ref, acc_ref[...], kh, tq)


def _gqa_online_kernel(qt_ref, k_ref, vt_ref, o_ref, m_ref, acc_ref, *, tq, nk):
    kh = pl.program_id(2)
    ki = pl.program_id(3)

    @pl.when(ki == 0)
    def _():
        m_ref[...] = jnp.full_like(m_ref, NEG_INF)
        acc_ref[...] = jnp.zeros_like(acc_ref)

    st = jnp.dot(k_ref[0, 0][:, :HEAD_DIM], qt_ref[0, 0, 0][:HEAD_DIM, :],
                 preferred_element_type=F32)
    m_prev = m_ref[0:1, :]
    m_new = jnp.maximum(m_prev, jnp.max(st, axis=0, keepdims=True))
    alpha = jnp.exp2(m_prev - m_new)
    pt = jnp.exp2(st - m_new).astype(BF16)
    acc_ref[...] = alpha * acc_ref[...] + jnp.dot(vt_ref[0, 0], pt, preferred_element_type=F32)
    m_ref[...] = jnp.broadcast_to(m_new, m_ref.shape)

    @pl.when(ki == nk - 1)
    def _():
        _gqa_finish(o_ref, acc_ref[...], kh, tq)


def _gqa_attend(qt, k, vt, fast, tq):
    B, _, T, _ = k.shape
    tk = min(8192, T) if fast else 512
    nq, nk = T // tq, T // tk
    n_cols = GQA_GROUP * tq
    acc = pltpu.VMEM((GQA_VT_ROWS, n_cols), F32)
    if fast:
        body = functools.partial(_gqa_fast_kernel, tq=tq, tk=tk, kc=256, nk=nk)
        scratch = [acc]
    else:
        body = functools.partial(_gqa_online_kernel, tq=tq, nk=nk)
        scratch = [pltpu.VMEM((SUBLANES, n_cols), F32), acc]
    return pl.pallas_call(
        body,
        grid=(B, nq, GQA_KV_HEADS, nk),
        in_specs=[pl.BlockSpec((1, 1, 1, LANES, n_cols), lambda b, i, h, j: (b, h, i, 0, 0)),
                  pl.BlockSpec((1, 1, tk, LANES), lambda b, i, h, j: (b, h, j, 0)),
                  pl.BlockSpec((1, 1, GQA_VT_ROWS, tk), lambda b, i, h, j: (b, h, 0, j))],
        out_specs=pl.BlockSpec((1, tq, GQA_W), lambda b, i, h, j: (b, i, 0)),
        out_shape=jax.ShapeDtypeStruct((B, T, GQA_W), BF16),
        scratch_shapes=scratch,
        compiler_params=_params(("parallel", "arbitrary", "arbitrary", "arbitrary")),
        name="gqa_fast" if fast else "gqa_online",
    )(qt, k, vt)


def _gqa(qt, k, vt, mb, tq):
    return lax.cond(mb <= GQA_SAFE_LOG2,
                    functools.partial(_gqa_attend, fast=True, tq=tq),
                    functools.partial(_gqa_attend, fast=False, tq=tq), qt, k, vt)


DIL_UNITS = 8


def _dil_kernel(q_ref, k_ref, v_ref, o_ref, lse_ref, *, S, bq, kw, nb, rb, dilation, slopes):
    base = pl.program_id(2) * (nb * bq)
    row = lax.broadcasted_iota(jnp.int32, (bq, kw), 0)
    col = lax.broadcasted_iota(jnp.int32, (bq, kw), 1)
    hid = _head_id((bq, LANES), 1)

    def unit(rr, n):
        p0 = base + n * bq
        if kw == S:
            start = 0
            kwin, vwin = k_ref[0, rr], v_ref[0, rr]
        else:
            start = pl.multiple_of(jnp.clip(p0 - DIL_SIDE, 0, S - kw), DIL_SIDE)
            kwin = k_ref[0, rr, pl.ds(start, kw), :]
            vwin = v_ref[0, rr, pl.ds(start, kw), :]
        q = q_ref[0, rr, n * bq:(n + 1) * bq, :]
        scores = [lax.dot_general(jnp.where(hid == hh, q, jnp.zeros_like(q)), kwin, (((1,), (1,)), ((), ())),
                                  preferred_element_type=F32) for hh in range(DIL_HEADS_PER_GROUP)]
        yield
        rel = jnp.abs(col - row + (start - p0))
        valid = rel <= DIL_SIDE
        dist = (rel * dilation).astype(F32)
        outs, lses = [], []
        for hh in range(DIL_HEADS_PER_GROUP):
            s = jnp.where(valid, scores[hh] * HEAD_DIM ** -0.5 - slopes[hh] * dist, NEG_INF)
            m = jnp.max(s, axis=-1, keepdims=True)
            p = jnp.exp(s - m)
            l = jnp.sum(p, axis=-1, keepdims=True)
            outs.append(jnp.dot(p.astype(BF16), vwin, preferred_element_type=F32) / l)
            lses.append(m + jnp.log(l))
        o_ref[0, rr, n * bq:(n + 1) * bq, :] = jnp.where(hid == 0, outs[0], outs[1]).astype(o_ref.dtype)
        lse_ref[0, rr, n * bq:(n + 1) * bq, :] = jnp.where(hid == 0, lses[0], lses[1])

    _run_interleaved([unit(rr, n) for rr in range(rb) for n in range(nb)])


def _dilated(qkv, dilation, slopes):
    B, d, S, _ = qkv.shape
    bq = min(128, S)
    kw = min(bq + 2 * DIL_SIDE, S)
    nb = min(DIL_UNITS, S // bq)
    rb = min(d, DIL_UNITS // nb)
    seq = lambda part: pl.BlockSpec((1, rb, S, LANES), lambda b, r, i: (b, r, 0, part))
    blk = lambda part: pl.BlockSpec((1, rb, nb * bq, LANES), lambda b, r, i: (b, r, i, part))
    return pl.pallas_call(
        functools.partial(_dil_kernel, S=S, bq=bq, kw=kw, nb=nb, rb=rb, dilation=dilation, slopes=slopes),
        grid=(B, d // rb, S // (nb * bq)),
        in_specs=[blk(0), seq(1), seq(2)],
        out_specs=[blk(0), blk(0)],
        out_shape=[jax.ShapeDtypeStruct((B, d, S, LANES), BF16), jax.ShapeDtypeStruct((B, d, S, LANES), F32)],
        compiler_params=_params(("parallel", "parallel", "arbitrary")),
        name=f"dilated_{dilation}",
    )(qkv, qkv, qkv)


def _outproj_kernel(x_ref, hf_ref, hb_ref, z_ref, b_ref, o0_ref, o1_ref, o2_ref, l0_ref, l1_ref, l2_ref,
                    gain_ref, w_ref, gpost_ref, out_ref, nat_ref, *, tm):
    z = z_ref[...].astype(F32)
    mo = z[:, :MLSTM_W]
    mz = z[:, MLSTM_W:2 * MLSTM_W]
    gz = z[:, 2 * MLSTM_W:2 * MLSTM_W + GQA_W]
    dz = z[:, 2 * MLSTM_W + GQA_W:]

    h = _sigmoid(mo) * (hf_ref[...] + hb_ref[...])
    a = h * lax.rsqrt(_head_sumsq(h) * (1.0 / HEAD_DIM) + NORM_EPS) * gain_ref[...] * _silu(mz)

    b = b_ref[...].astype(F32) * _silu(gz)

    n_groups = len(DIL_PATTERNS)
    for g, (o_ref, l_ref, (_, d)) in enumerate(zip((o0_ref, o1_ref, o2_ref), (l0_ref, l1_ref, l2_ref), DIL_PATTERNS)):
        for r in range(d):
            dst = slice(None) if d == 1 else pl.ds(r, tm // d, stride=d)
            nat_ref[g, dst, :] = o_ref[0, r].astype(F32)
            nat_ref[n_groups + g, dst, :] = l_ref[0, r]
    outs = [nat_ref[g] for g in range(n_groups)]
    lses = [nat_ref[n_groups + g] for g in range(n_groups)]
    mx = jnp.maximum(jnp.maximum(lses[0], lses[1]), lses[2])
    es = [jnp.exp(l - mx) for l in lses]
    inv = 1.0 / (es[0] + es[1] + es[2])
    c = jnp.concatenate([outs[g] * (es[g] * inv) for g in range(3)], axis=1) * _silu(dz)

    w = w_ref[...]
    y = jnp.dot(a.astype(BF16), w[:MLSTM_W], preferred_element_type=F32)
    y += jnp.dot(b.astype(BF16), w[MLSTM_W:MLSTM_W + GQA_W], preferred_element_type=F32)
    y += jnp.dot(c.astype(BF16), w[MLSTM_W + GQA_W:], preferred_element_type=F32)
    ms = jnp.mean(y * y, axis=-1, keepdims=True)
    out_ref[...] = x_ref[...] + y * lax.rsqrt(ms + NORM_EPS) * gpost_ref[...]


def _outproj(x, hf, hb, z, b, dil_o, dil_l, gain, w, gpost, T, tm=TOKEN_TILE):
    n = x.shape[0]
    tps = T // tm
    row = lambda a: pl.BlockSpec((tm, a.shape[1]), lambda i: (i, 0))
    full = lambda a: pl.BlockSpec(a.shape, lambda i: (0, 0))
    cls = lambda a: pl.BlockSpec((1, a.shape[1], tm // a.shape[1], LANES), lambda i: (i // tps, 0, i % tps, 0))
    rows = [x, hf, hb, z, b]
    dil = list(dil_o) + list(dil_l)
    consts = [gain, w, gpost]
    return pl.pallas_call(
        functools.partial(_outproj_kernel, tm=tm),
        grid=(n // tm,),
        in_specs=[row(a) for a in rows] + [cls(a) for a in dil] + [full(a) for a in consts],
        out_specs=pl.BlockSpec((tm, D_MODEL), lambda i: (i, 0)),
        out_shape=jax.ShapeDtypeStruct((n, D_MODEL), F32),
        scratch_shapes=[pltpu.VMEM((2 * len(DIL_PATTERNS), tm, LANES), F32)],
        compiler_params=_params(("parallel",)),
        name="outproj",
    )(*rows, *dil, *consts)


def _rope_tables(T):
    rows = T // GRID_W
    r, c = jnp.meshgrid(jnp.arange(rows), jnp.arange(GRID_W), indexing="ij")
    r = r.reshape(-1).astype(F32)
    c = c.reshape(-1).astype(F32)
    axis_dim = HEAD_DIM // 2
    inv = ROPE_THETA ** (-jnp.arange(0, axis_dim, 2, dtype=F32) / axis_dim)
    ang = jnp.concatenate([r[:, None] * inv, c[:, None] * inv], axis=-1)
    cos, sin = jnp.cos(ang), jnp.sin(ang)
    cos_t = jnp.tile(jnp.concatenate([cos, cos], axis=-1), (1, LANES // HEAD_DIM))
    sin_t = jnp.tile(jnp.concatenate([-sin, sin], axis=-1), (1, LANES // HEAD_DIM))
    return cos_t, sin_t


def _split_w_in(w):
    cuts = np.cumsum((0,) + IN_SPLITS)
    mq, mk, mv, mo, mg, mz, gq, gk, gv, gz, dq, dk, dv, dz = [w[:, cuts[i]:cuts[i + 1]] for i in range(14)]
    pad = jnp.zeros((w.shape[0], GATE_PAD - mg.shape[1]), w.dtype)
    return jnp.concatenate([mq, mk, mv, mg, pad, gq, gk, gv, dq, dk, dv, mo, mz, gz, dz], axis=1).astype(BF16)


def _alibi_slopes():
    return [float(np.exp2(np.float32(-8.0) * np.float32(i) / np.float32(DIL_HEADS)))
            for i in range(1, DIL_HEADS + 1)]


def _layer(x, B, T, p, cos, sin, slopes):
    n = B * T
    mb = _gqa_bound(p["q_norm"], p["k_norm"])
    m_qkv, m_gates, qt, k, vt, *dil_qkv, z = _inproj(
        x, p["norm_pre"], p["w_in"], p["conv_w"], p["conv_b"], cos, sin, p["q_norm"], p["k_norm"],
        jnp.full((1, LANES), -mb, F32), B, T, TOKEN_TILE)
    hf, hb = _mlstm(m_qkv.reshape(B, T, 3 * MLSTM_W), m_gates.reshape(B, T, GATE_PAD), p["gate_bias"])
    b = _gqa(qt, k, vt, mb, TOKEN_TILE)

    dil_o, dil_l = [], []
    for g, (_, dilation) in enumerate(DIL_PATTERNS):
        o_g, l_g = _dilated(dil_qkv[g], dilation, slopes[g * DIL_HEADS_PER_GROUP:(g + 1) * DIL_HEADS_PER_GROUP])
        dil_o.append(o_g)
        dil_l.append(l_g)

    return _outproj(x, hf.reshape(n, MLSTM_W), hb.reshape(n, MLSTM_W), z, b.reshape(n, GQA_W),
                    dil_o, dil_l, p["out_gain"], p["w_out"], p["norm_post"], T)


def _trunk(x, layers):
    B, T, _ = x.shape
    cos, sin = _rope_tables(T)
    slopes = _alibi_slopes()
    x = x.reshape(B * T, D_MODEL)
    for p in layers:
        x = _layer(x, B, T, p, cos, sin, slopes)
    return x.reshape(B, T, D_MODEL)


def kernel(x_prompt, x_sample, norm_pre, w_in, mlstm_gate_bias, mlstm_conv_w, mlstm_conv_b, mlstm_out_gain,
           gqa_q_norm, gqa_k_norm, w_out, norm_post):
    depth = w_in.shape[0]
    row = lambda a: a.reshape(1, -1).astype(F32)
    layers = []
    for l in range(depth):
        gate_bias = jnp.concatenate(
            [mlstm_gate_bias[l].astype(F32), jnp.zeros((GATE_PAD - 4 * MLSTM_HEADS,), F32)]).reshape(1, GATE_PAD)
        layers.append(dict(
            norm_pre=row(norm_pre[l]),
            w_in=_split_w_in(w_in[l]),
            gate_bias=gate_bias,
            conv_w=mlstm_conv_w[l].astype(F32),
            conv_b=row(mlstm_conv_b[l]),
            out_gain=row(mlstm_out_gain[l]),
            q_norm=row(jnp.tile(gqa_q_norm[l], GQA_Q_HEADS)),
            k_norm=row(jnp.tile(gqa_k_norm[l], GQA_KV_HEADS)),
            w_out=w_out[l].astype(BF16),
            norm_post=row(norm_post[l]),
        ))
    return (_trunk(x_prompt, layers), _trunk(x_sample, layers))
```

```python
import functools
import math

import numpy as np
import jax
import jax.numpy as jnp
from jax import lax
from jax.experimental import pallas as pl
from jax.experimental.pallas import tpu as pltpu

F32 = jnp.float32
BF16 = jnp.bfloat16

D_MODEL = 1024
HEAD_DIM = 64
NORM_EPS = 1e-6
GRID_W = 64
ROPE_THETA = 10000.0
MLSTM_HEADS = 4
MLSTM_W = MLSTM_HEADS * HEAD_DIM
MLSTM_CHUNK = 128
GQA_Q_HEADS = 6
GQA_KV_HEADS = 2
GQA_GROUP = GQA_Q_HEADS // GQA_KV_HEADS
GQA_W = GQA_Q_HEADS * HEAD_DIM
GQA_KV_W = GQA_KV_HEADS * HEAD_DIM
DIL_PATTERNS = ((128, 1), (512, 4), (2048, 16))
DIL_HEADS_PER_GROUP = 2
DIL_HEADS = DIL_HEADS_PER_GROUP * len(DIL_PATTERNS)
DIL_W = DIL_HEADS * HEAD_DIM
DIL_SIDE = 64
IN_SPLITS = (MLSTM_W, MLSTM_W, MLSTM_W, MLSTM_W, 4 * MLSTM_HEADS, MLSTM_W,
             GQA_W, GQA_KV_W, GQA_KV_W, GQA_W, DIL_W, DIL_W, DIL_W, DIL_W)

LANES = 128
SUBLANES = 8
V7X_VMEM_BYTES = 64 * 1024 * 1024
GATE_PAD = LANES
G1_W = GQA_W + 2 * GQA_KV_W
D1_W = 3 * DIL_W
Z_W = 2 * MLSTM_W + GQA_W + DIL_W
TOKEN_TILE = 512
VMEM_LIMIT = 3 * V7X_VMEM_BYTES // 4
LOG2E = math.log2(math.e)
NEG_INF = float("-inf")


def _sigmoid(x):
    return 1.0 / (1.0 + jnp.exp(-x))


def _silu(x):
    return x * _sigmoid(x)


def _log_sigmoid(x):
    return jnp.minimum(x, 0.0) - jnp.log(1.0 + jnp.exp(-jnp.abs(x)))


def _head_id(shape, axis):
    return lax.broadcasted_iota(jnp.int32, shape, axis) // HEAD_DIM


def _run_interleaved(stage_generators):
    live = list(stage_generators)
    while live:
        live = [g for g in live if next(g, StopIteration) is not StopIteration]


def _params(sem):
    return pltpu.CompilerParams(dimension_semantics=sem, vmem_limit_bytes=VMEM_LIMIT)


def _split_terms(a, terms):
    out = []
    for t in range(terms):
        hi = a.astype(BF16)
        out.append(hi)
        if t + 1 < terms:
            a = a - hi.astype(F32)
    return out


def _inproj_kernel(x_ref, xp_ref, xn_ref, g_ref, wm_ref, wg_ref, wd_ref, wz_ref, cw_ref, cb_ref,
                   cos_ref, sin_ref, qn_ref, kn_ref, nmb_ref,
                   oqkv_ref, ogate_ref, oqt_ref, ok_ref, ovt_ref, od1_ref, od4_ref, od16_ref, oz_ref, ds_ref,
                   *, tm, tps):
    def normed(x):
        ms = jnp.mean(x * x, axis=-1, keepdims=True)
        return (x * lax.rsqrt(ms + NORM_EPS) * g_ref[...]).astype(BF16)

    h = normed(x_ref[...])
    halo = normed(jnp.concatenate([xp_ref[...], xn_ref[...]], axis=0))
    ymh = jnp.dot(jnp.concatenate([h, halo], axis=0), wm_ref[...], preferred_element_type=F32)
    ym = ymh[:tm]
    qkw = 2 * MLSTM_W
    yh = ymh[tm:, :qkw]
    t = pl.program_id(0) % tps
    prev_row = yh[SUBLANES - 1:SUBLANES, :] * jnp.where(t > 0, 1.0, 0.0).astype(F32)
    next_row = yh[SUBLANES:SUBLANES + 1, :] * jnp.where(t < tps - 1, 1.0, 0.0).astype(F32)
    qk = ym[:, :qkw]
    rid = lax.broadcasted_iota(jnp.int32, (tm, qkw), 0)
    qk_m1 = jnp.where(rid == 0, prev_row, pltpu.roll(qk, 1, 0))
    qk_p1 = jnp.where(rid == tm - 1, next_row, pltpu.roll(qk, tm - 1, 0))
    cw = cw_ref[...]
    act = _silu(qk_m1 * cw[0:1, :] + qk * cw[1:2, :] + qk_p1 * cw[2:3, :] + cb_ref[...])
    oqkv_ref[:, :MLSTM_W] = act[:, :MLSTM_W].astype(BF16)
    oqkv_ref[:, MLSTM_W:qkw] = (act[:, MLSTM_W:] * HEAD_DIM ** -0.5).astype(BF16)
    oqkv_ref[:, qkw:] = ym[:, qkw:3 * MLSTM_W].astype(BF16)
    ogate_ref[...] = ym[:, 3 * MLSTM_W:]
    _gqa_prep_tile(jnp.dot(h, wg_ref[...], preferred_element_type=F32), cos_ref, sin_ref, qn_ref, kn_ref, nmb_ref,
                   oqt_ref, ok_ref, ovt_ref)
    oz_ref[...] = jnp.dot(h, wz_ref[...], preferred_element_type=F32).astype(BF16)
    yd = jnp.dot(h, wd_ref[...], preferred_element_type=F32)
    n_tiles = D1_W // LANES
    for j in range(n_tiles):
        ds_ref[j] = yd[:, j * LANES:(j + 1) * LANES]
    for g, (out_ref, (_, d)) in enumerate(zip((od1_ref, od4_ref, od16_ref), DIL_PATTERNS)):
        for part in range(3):
            src = part * len(DIL_PATTERNS) + g
            for r in range(d):
                rows = ds_ref[src] if d == 1 else ds_ref[src, pl.ds(r, tm // d, stride=d), :]
                out_ref[0, r, :, part * LANES:(part + 1) * LANES] = rows.astype(BF16)


def _inproj(x, g, wm, wg, wd, wz, conv_w, conv_b, cos, sin, q_norm, k_norm, neg_mb, B, T, tm):
    n = x.shape[0]
    tps = T // tm
    hb = tm // SUBLANES
    row = lambda w: pl.BlockSpec((tm, w), lambda i: (i, 0))
    full = lambda a: pl.BlockSpec(a.shape, lambda i: (0, 0))
    pos = pl.BlockSpec((tm, LANES), lambda i: (i % tps, 0))
    gqa_specs = [pl.BlockSpec((1, GQA_KV_HEADS, 1, LANES, GQA_GROUP * tm), lambda i: (i // tps, 0, i % tps, 0, 0)),
                 pl.BlockSpec((1, GQA_KV_HEADS, tm, LANES), lambda i: (i // tps, 0, i % tps, 0)),
                 pl.BlockSpec((1, GQA_KV_HEADS, GQA_VT_ROWS, tm), lambda i: (i // tps, 0, 0, i % tps))]
    gqa_shapes = [jax.ShapeDtypeStruct((B, GQA_KV_HEADS, tps, LANES, GQA_GROUP * tm), BF16),
                  jax.ShapeDtypeStruct((B, GQA_KV_HEADS, T, LANES), BF16),
                  jax.ShapeDtypeStruct((B, GQA_KV_HEADS, GQA_VT_ROWS, T), BF16)]
    dil_spec = lambda d: pl.BlockSpec((1, d, tm // d, 3 * LANES), lambda i: (i // tps, 0, i % tps, 0))
    dil_shape = lambda d: jax.ShapeDtypeStruct((B, d, T // d, 3 * LANES), BF16)
    dils = [d for _, d in DIL_PATTERNS]
    halo_prev = pl.BlockSpec((SUBLANES, D_MODEL), lambda i: (jnp.maximum(i * hb - 1, 0), 0))
    halo_next = pl.BlockSpec((SUBLANES, D_MODEL), lambda i: (jnp.minimum((i + 1) * hb, n // SUBLANES - 1), 0))
    return pl.pallas_call(
        functools.partial(_inproj_kernel, tm=tm, tps=tps),
        grid=(n // tm,),
        in_specs=[row(D_MODEL), halo_prev, halo_next, full(g), full(wm), full(wg), full(wd), full(wz),
                  full(conv_w), full(conv_b), pos, pos, full(q_norm), full(k_norm), full(neg_mb)],
        out_specs=[row(3 * MLSTM_W), row(GATE_PAD)] + gqa_specs + [dil_spec(d) for d in dils] + [row(Z_W)],
        out_shape=[jax.ShapeDtypeStruct((n, 3 * MLSTM_W), BF16), jax.ShapeDtypeStruct((n, GATE_PAD), F32)]
        + gqa_shapes + [dil_shape(d) for d in dils] + [jax.ShapeDtypeStruct((n, Z_W), BF16)],
        scratch_shapes=[pltpu.VMEM((D1_W // LANES, tm, LANES), F32)],
        compiler_params=_params(("parallel",)),
        name="inproj",
    )(x, x, x, g, wm, wg, wd, wz, conv_w, conv_b, cos, sin, q_norm, k_norm, neg_mb)


MLSTM_EXP2_CAP = 126.0


def _mlstm_consts():
    L, W, H = MLSTM_CHUNK, MLSTM_W, MLSTM_HEADS
    r = np.arange(L)
    tri = np.stack([r[None, :] <= r[:, None], r[None, :] >= r[:, None]]).astype(np.float32)
    lane_head = np.arange(W) // HEAD_DIM
    spread = np.zeros((2, GATE_PAD, W), np.float32)
    neg = np.full((SUBLANES, GATE_PAD), NEG_INF, np.float32)
    for d in range(2):
        fo = (2 * d + 1) * H
        for h in range(H):
            spread[d, fo + h] = lane_head == h
        neg[d, fo:fo + H] = 0.0
    ones_stack = (np.repeat(np.arange(H), L)[:, None] == lane_head[None, :]).astype(np.float32)
    same_head = (lane_head[:, None] == lane_head[None, :]).astype(np.float32)
    return (jnp.asarray(tri, BF16), jnp.asarray(spread, BF16), jnp.asarray(neg), jnp.asarray(ones_stack, BF16),
            jnp.asarray(same_head, BF16), jnp.asarray(same_head))


def _mlstm_chunk(qkv_ref, gate_ref, out_ref, c_ref, n_ref, m_ref, gb_ref, tri_ref, spread_ref, neg_ref,
                 os_ref, shb_ref, shf_ref, carry, bi, d, row0, first, last):
    forward = d == 0
    st = 2 * bi + d
    L, W, H = MLSTM_CHUNK, MLSTM_W, MLSTM_HEADS
    rows = slice(row0, row0 + L)
    x = qkv_ref[bi, rows, :]
    q, k, v = x[:, :W], x[:, W:2 * W], x[:, 2 * W:]
    g = gate_ref[bi, rows, :] + gb_ref[...]
    fo = (2 * d + 1) * H
    s_heads = [lax.dot_general(q * os_ref[h * L:(h + 1) * L, :], k, (((1,), (1,)), ((), ())),
                               preferred_element_type=F32) for h in range(H)]
    cum = sum(jnp.dot(tri_ref[d], t, preferred_element_type=F32)
              for t in _split_terms(_log_sigmoid(g) * LOG2E, 3))

    def read_state(c_state, n_state):
        q_n = jnp.dot((q.astype(F32) * n_state).astype(BF16), shb_ref[...], preferred_element_type=F32)
        return c_state, n_state, jnp.dot(q, c_state.astype(BF16), preferred_element_type=F32), q_n

    early = first and last
    if early:
        c_state, n_state, q_c, q_n = read_state(c_ref[st], n_ref[st][0:1, :])
    yield
    u = pltpu.roll(g * LOG2E, H, 1) - cum
    u_t = u.T
    r_i = lax.broadcasted_iota(jnp.int32, (L, L), 0)
    c_i = lax.broadcasted_iota(jnp.int32, (L, L), 1)
    seen = (c_i <= r_i) if forward else (c_i >= r_i)
    gl = lax.broadcasted_iota(jnp.int32, (L, GATE_PAD), 1)
    m_state = m_ref[st][0:1, :] if first else carry[st, "m"]
    m_row = jnp.zeros((L, GATE_PAD), F32)
    w_blocks = []
    for h in range(H):
        col = fo + h
        bc = cum[:, col:col + 1]
        dmat = jnp.where(seen, bc + u_t[col:col + 1, :], NEG_INF)
        m_row_h = jnp.maximum(bc + m_state[:, col:col + 1], jnp.max(dmat, axis=1, keepdims=True))
        w_blocks.append((jnp.exp2(dmat - m_row_h) * s_heads[h]).astype(BF16))
        m_row = jnp.where(gl == col, m_row_h, m_row)

    neg = neg_ref[d:d + 1, :]
    tot_row = L - 1 if forward else 0
    b_tot = cum[tot_row:tot_row + 1, :]
    gv = b_tot + u
    m_new = jnp.maximum(b_tot + m_state, jnp.max(gv, axis=0, keepdims=True))
    carry[st, "m"] = jnp.where(neg == 0.0, m_new, 0.0)
    decay = jnp.exp2(b_tot + m_state - m_new + neg)
    stack = jnp.concatenate([jnp.exp2(cum + m_state - m_row + neg),
                             jnp.exp2(jnp.minimum(-m_row, MLSTM_EXP2_CAP) + neg),
                             jnp.exp2(gv - m_new + neg),
                             jnp.broadcast_to(decay, (SUBLANES, GATE_PAD))], axis=0)
    spread = sum(jnp.dot(t, spread_ref[d], preferred_element_type=F32) for t in _split_terms(stack, 2))
    w_inter, inv_floor, k_fac, decay_e = spread[:L], spread[L:2 * L], spread[2 * L:3 * L], spread[3 * L:3 * L + 1]
    yield
    if not first:
        yield
        c_state, n_state, q_c, q_n = read_state(carry[st, "c"], carry[st, "n"])
    elif not early:
        c_state, n_state, q_c, q_n = read_state(c_ref[st], n_ref[st][0:1, :])
    w_cat = jnp.concatenate(w_blocks, axis=1)
    v_stack = os_ref[...] * jnp.concatenate([v] * H, axis=0)
    num = w_inter * q_c + jnp.dot(w_cat, v_stack, preferred_element_type=F32)
    den = w_inter * q_n + jnp.dot(w_cat, os_ref[...], preferred_element_type=F32)
    out_ref[bi, rows, :] = num / jnp.maximum(jnp.abs(den), inv_floor)
    yield
    kw = k_fac.astype(BF16) * k
    kv = lax.dot_general(kw, v, (((0,), (0,)), ((), ())), preferred_element_type=F32)
    k_sum = jnp.dot(jnp.ones((2 * SUBLANES, L), BF16), kw, preferred_element_type=F32)[0:1, :]
    carry[st, "c"] = decay_e * c_state + jnp.where(shf_ref[...] != 0.0, kv, 0.0)
    carry[st, "n"] = decay_e * n_state + k_sum
    if last:
        c_ref[st] = carry[st, "c"]
        n_ref[st] = jnp.broadcast_to(carry[st, "n"], (SUBLANES, W))
        m_ref[st] = jnp.broadcast_to(carry[st, "m"], (SUBLANES, GATE_PAD))


def _mlstm_kernel(qf_ref, gf_ref, qb_ref, gbk_ref, gb_ref, tri_ref, spread_ref, neg_ref, os_ref, shb_ref, shf_ref,
                  hf_ref, hb_ref, c_ref, n_ref, m_ref, *, bb, cpb):
    @pl.when(pl.program_id(1) == 0)
    def _():
        c_ref[...] = jnp.zeros_like(c_ref)
        n_ref[...] = jnp.zeros_like(n_ref)
        m_ref[...] = jnp.zeros_like(m_ref)

    consts = (gb_ref, tri_ref, spread_ref, neg_ref, os_ref, shb_ref, shf_ref)
    state = (c_ref, n_ref, m_ref)
    carry = {}
    chains = []
    for pos in range(cpb):
        for bi in range(bb):
            edge = (pos == 0, pos == cpb - 1)
            chains.append(_mlstm_chunk(qf_ref, gf_ref, hf_ref, *state, *consts, carry, bi, 0,
                                       pos * MLSTM_CHUNK, *edge))
            chains.append(_mlstm_chunk(qb_ref, gbk_ref, hb_ref, *state, *consts, carry, bi, 1,
                                       (cpb - 1 - pos) * MLSTM_CHUNK, *edge))
    _run_interleaved(chains)


def _mlstm(qkv, gates, gate_bias):
    B, T, _ = qkv.shape
    L = MLSTM_CHUNK
    bb = 2 if B % 2 == 0 else 1
    cpb = 2 if bb == 1 and (T // L) % 2 == 0 else 1
    nc = T // (L * cpb)
    consts = (gate_bias,) + _mlstm_consts()
    full = lambda a: pl.BlockSpec(a.shape, lambda b, c: (0,) * a.ndim)
    chunk = lambda w, pos: pl.BlockSpec((bb, cpb * L, w), lambda b, c: (b, pos(c), 0))
    fwd = lambda c: c
    bwd = lambda c: nc - 1 - c
    return pl.pallas_call(
        functools.partial(_mlstm_kernel, bb=bb, cpb=cpb),
        grid=(B // bb, nc),
        in_specs=[chunk(3 * MLSTM_W, fwd), chunk(GATE_PAD, fwd), chunk(3 * MLSTM_W, bwd), chunk(GATE_PAD, bwd)]
        + [full(a) for a in consts],
        out_specs=[chunk(MLSTM_W, fwd), chunk(MLSTM_W, bwd)],
        out_shape=[jax.ShapeDtypeStruct((B, T, MLSTM_W), F32)] * 2,
        scratch_shapes=[pltpu.VMEM((2 * bb, MLSTM_W, MLSTM_W), F32),
                        pltpu.VMEM((2 * bb, SUBLANES, MLSTM_W), F32),
                        pltpu.VMEM((2 * bb, SUBLANES, GATE_PAD), F32)],
        compiler_params=_params(("parallel", "arbitrary")),
        name="mlstm",
    )(qkv, gates, qkv, gates, *consts)


GQA_SAFE_LOG2 = 56.0
GQA_BOUND_SLACK = 1.0 + 2.0 ** -6
GQA_SCALE = HEAD_DIM ** -0.5 * LOG2E
GQA_VT_ROWS = LANES


def _rope_tile(a, cos, sin):
    half = HEAD_DIM // 2
    lane = lax.broadcasted_iota(jnp.int32, a.shape, 1) % HEAD_DIM
    rot = jnp.where(lane < half, pltpu.roll(a, LANES - half, 1), pltpu.roll(a, half, 1))
    return a * cos + rot * sin


def _head_sumsq(a):
    w = a.shape[1]
    same_head = jnp.where(_head_id((w, w), 0) == _head_id((w, w), 1), 1.0, 0.0).astype(BF16)
    return sum(jnp.dot(t, same_head, preferred_element_type=F32) for t in _split_terms(a * a, 2))


def _head_rms(a, gain):
    return a * lax.rsqrt(_head_sumsq(a) * (1.0 / HEAD_DIM) + NORM_EPS) * gain


def _gqa_prep_tile(x, cos_ref, sin_ref, qn_ref, kn_ref, nmb_ref, qt_out, k_out, vt_out):
    cos = cos_ref[...]
    sin = sin_ref[...]
    q = _head_rms(x[:, :GQA_W], qn_ref[...])
    k = _head_rms(x[:, GQA_W:GQA_W + GQA_KV_W], kn_ref[...])
    v = x[:, GQA_W + GQA_KV_W:]
    lane = lax.broadcasted_iota(jnp.int32, (x.shape[0], LANES), 1)
    first = lane < HEAD_DIM
    bound_col = jnp.where(lane == HEAD_DIM, nmb_ref[...], 0.0)
    ones_col = jnp.where(lane == HEAD_DIM, 1.0, 0.0)
    q_t = []
    for j in range(GQA_W // LANES):
        pair = _rope_tile(q[:, j * LANES:(j + 1) * LANES], cos, sin) * GQA_SCALE
        q_t.append(jnp.where(first, pair, bound_col).T.astype(BF16))
        q_t.append(jnp.where(first, pltpu.roll(pair, HEAD_DIM, 1), bound_col).T.astype(BF16))
    for kk in range(GQA_KV_HEADS):
        qt_out[0, kk, 0] = jnp.concatenate(q_t[kk * GQA_GROUP:(kk + 1) * GQA_GROUP], axis=1)
    k_rot = _rope_tile(k, cos, sin)
    k_out[0, 0] = jnp.where(first, k_rot, ones_col).astype(BF16)
    k_out[0, 1] = jnp.where(first, pltpu.roll(k_rot, HEAD_DIM, 1), ones_col).astype(BF16)
    vt_out[0, 0] = jnp.where(first, v, 1.0).T[:GQA_VT_ROWS].astype(BF16)
    vt_out[0, 1] = jnp.where(first, pltpu.roll(v, HEAD_DIM, 1), 1.0).T[:GQA_VT_ROWS].astype(BF16)


def _gqa_bound(q_norm, k_norm):
    return (HEAD_DIM * GQA_SCALE * GQA_BOUND_SLACK) * jnp.max(jnp.abs(q_norm)) * jnp.max(jnp.abs(k_norm))


def _gqa_finish(o_ref, acc_t, kh, tq, qb):
    o_t = acc_t[:HEAD_DIM, :] / acc_t[HEAD_DIM:HEAD_DIM + 1, :]
    o = jnp.concatenate([o_t, o_t], axis=0).T
    for kk in range(GQA_KV_HEADS):
        @pl.when(kh == kk)
        def _():
            for t in range(qb):
                for h in range(GQA_GROUP):
                    lo = (kk * GQA_GROUP + h) * HEAD_DIM
                    r0 = (t * GQA_GROUP + h) * tq
                    o_ref[0, t * tq:(t + 1) * tq, lo:lo + HEAD_DIM] = o[r0:r0 + tq, :HEAD_DIM].astype(o_ref.dtype)


def _gqa_queries(qt_ref, qb):
    return jnp.concatenate([qt_ref[0, 0, t] for t in range(qb)], axis=1) if qb > 1 else qt_ref[0, 0, 0]


def _gqa_fast_kernel(qt_ref, k_ref, vt_ref, o_ref, acc_ref, *, tq, qb, tk, kc, nk):
    kh = pl.program_id(2)
    ki = pl.program_id(3)

    @pl.when(ki == 0)
    def _():
        acc_ref[...] = jnp.zeros_like(acc_ref)

    qt = _gqa_queries(qt_ref, qb)
    scores = lambda c: jnp.dot(k_ref[0, 0, c * kc:(c + 1) * kc, :], qt, preferred_element_type=F32)
    n_chunks = tk // kc
    acc = None
    st_next = scores(0)
    for c in range(n_chunks):
        st = st_next
        if c + 1 < n_chunks:
            st_next = scores(c + 1)
        pt = jnp.exp2(st).astype(BF16)
        d = jnp.dot(vt_ref[0, 0, :, c * kc:(c + 1) * kc], pt, preferred_element_type=F32)
        acc = d if acc is None else acc + d
    acc_ref[...] += acc

    @pl.when(ki == nk - 1)
    def _():
        _gqa_finish(o_ref, acc_ref[...], kh, tq, qb)


def _gqa_online_kernel(qt_ref, k_ref, vt_ref, o_ref, m_ref, acc_ref, *, tq, qb, nk):
    kh = pl.program_id(2)
    ki = pl.program_id(3)

    @pl.when(ki == 0)
    def _():
        m_ref[...] = jnp.full_like(m_ref, NEG_INF)
        acc_ref[...] = jnp.zeros_like(acc_ref)

    st = jnp.dot(k_ref[0, 0][:, :HEAD_DIM], _gqa_queries(qt_ref, qb)[:HEAD_DIM, :],
                 preferred_element_type=F32)
    m_prev = m_ref[0:1, :]
    m_new = jnp.maximum(m_prev, jnp.max(st, axis=0, keepdims=True))
    alpha = jnp.exp2(m_prev - m_new)
    pt = jnp.exp2(st - m_new).astype(BF16)
    acc_ref[...] = alpha * acc_ref[...] + jnp.dot(vt_ref[0, 0], pt, preferred_element_type=F32)
    m_ref[...] = jnp.broadcast_to(m_new, m_ref.shape)

    @pl.when(ki == nk - 1)
    def _():
        _gqa_finish(o_ref, acc_ref[...], kh, tq, qb)


def _gqa_attend(qt, k, vt, fast, tq):
    B, _, T, _ = k.shape
    tk = min(8192, T) if fast else 512
    nq, nk = T // tq, T // tk
    qb = 2 if fast and nq % 2 == 0 else 1
    n_cols = qb * GQA_GROUP * tq
    acc = pltpu.VMEM((GQA_VT_ROWS, n_cols), F32)
    if fast:
        body = functools.partial(_gqa_fast_kernel, tq=tq, qb=qb, tk=tk, kc=256, nk=nk)
        scratch = [acc]
    else:
        body = functools.partial(_gqa_online_kernel, tq=tq, qb=qb, nk=nk)
        scratch = [pltpu.VMEM((SUBLANES, n_cols), F32), acc]
    return pl.pallas_call(
        body,
        grid=(B, nq // qb, GQA_KV_HEADS, nk),
        in_specs=[pl.BlockSpec((1, 1, qb, LANES, GQA_GROUP * tq), lambda b, i, h, j: (b, h, i, 0, 0)),
                  pl.BlockSpec((1, 1, tk, LANES), lambda b, i, h, j: (b, h, j, 0)),
                  pl.BlockSpec((1, 1, GQA_VT_ROWS, tk), lambda b, i, h, j: (b, h, 0, j))],
        out_specs=pl.BlockSpec((1, qb * tq, GQA_W), lambda b, i, h, j: (b, i, 0)),
        out_shape=jax.ShapeDtypeStruct((B, T, GQA_W), BF16),
        scratch_shapes=scratch,
        compiler_params=_params(("parallel", "arbitrary", "arbitrary", "arbitrary")),
        name="gqa_fast" if fast else "gqa_online",
    )(qt, k, vt)


def _gqa(qt, k, vt, mb, tq):
    return lax.cond(mb <= GQA_SAFE_LOG2,
                    functools.partial(_gqa_attend, fast=True, tq=tq),
                    functools.partial(_gqa_attend, fast=False, tq=tq), qt, k, vt)


DIL_UNITS = 8


def _dil_kernel(q_ref, k_ref, v_ref, o_ref, lse_ref, *, S, bq, kw, nb, rb, dilation, slopes):
    base = pl.program_id(2) * (nb * bq)
    row = lax.broadcasted_iota(jnp.int32, (bq, kw), 0)
    col = lax.broadcasted_iota(jnp.int32, (bq, kw), 1)
    hid = _head_id((bq, LANES), 1)

    def unit(rr, n):
        p0 = base + n * bq
        if kw == S:
            start = 0
            kwin, vwin = k_ref[0, rr], v_ref[0, rr]
        else:
            start = pl.multiple_of(jnp.clip(p0 - DIL_SIDE, 0, S - kw), DIL_SIDE)
            kwin = k_ref[0, rr, pl.ds(start, kw), :]
            vwin = v_ref[0, rr, pl.ds(start, kw), :]
        q = q_ref[0, rr, n * bq:(n + 1) * bq, :]
        scores = [lax.dot_general(jnp.where(hid == hh, q, jnp.zeros_like(q)), kwin, (((1,), (1,)), ((), ())),
                                  preferred_element_type=F32) for hh in range(DIL_HEADS_PER_GROUP)]
        yield
        rel = jnp.abs(col - row + (start - p0))
        valid = rel <= DIL_SIDE
        dist = (rel * dilation).astype(F32)
        outs, lses = [], []
        for hh in range(DIL_HEADS_PER_GROUP):
            s = jnp.where(valid, scores[hh] * HEAD_DIM ** -0.5 - slopes[hh] * dist, NEG_INF)
            m = jnp.max(s, axis=-1, keepdims=True)
            p = jnp.exp(s - m)
            l = jnp.sum(p, axis=-1, keepdims=True)
            outs.append(jnp.dot(p.astype(BF16), vwin, preferred_element_type=F32) / l)
            lses.append(m + jnp.log(l))
        o_ref[0, rr, n * bq:(n + 1) * bq, :] = jnp.where(hid == 0, outs[0], outs[1]).astype(o_ref.dtype)
        lse_ref[0, rr, n * bq:(n + 1) * bq, :] = jnp.where(hid == 0, lses[0], lses[1])

    _run_interleaved([unit(rr, n) for rr in range(rb) for n in range(nb)])


def _dilated(qkv, dilation, slopes):
    B, d, S, _ = qkv.shape
    bq = min(128, S)
    kw = min(bq + 2 * DIL_SIDE, S)
    nb = min(DIL_UNITS, S // bq)
    rb = min(d, DIL_UNITS // nb)
    seq = lambda part: pl.BlockSpec((1, rb, S, LANES), lambda b, r, i: (b, r, 0, part))
    blk = lambda part: pl.BlockSpec((1, rb, nb * bq, LANES), lambda b, r, i: (b, r, i, part))
    return pl.pallas_call(
        functools.partial(_dil_kernel, S=S, bq=bq, kw=kw, nb=nb, rb=rb, dilation=dilation, slopes=slopes),
        grid=(B, d // rb, S // (nb * bq)),
        in_specs=[blk(0), seq(1), seq(2)],
        out_specs=[blk(0), blk(0)],
        out_shape=[jax.ShapeDtypeStruct((B, d, S, LANES), BF16), jax.ShapeDtypeStruct((B, d, S, LANES), F32)],
        compiler_params=_params(("parallel", "parallel", "arbitrary")),
        name=f"dilated_{dilation}",
    )(qkv, qkv, qkv)


def _outproj_kernel(x_ref, hf_ref, hb_ref, z_ref, b_ref, o0_ref, o1_ref, o2_ref, l0_ref, l1_ref, l2_ref,
                    gain_ref, w_ref, gpost_ref, out_ref, nat_ref, *, tm):
    z = z_ref[...].astype(F32)
    mo = z[:, :MLSTM_W]
    mz = z[:, MLSTM_W:2 * MLSTM_W]
    gz = z[:, 2 * MLSTM_W:2 * MLSTM_W + GQA_W]
    dz = z[:, 2 * MLSTM_W + GQA_W:]

    h = _sigmoid(mo) * (hf_ref[...] + hb_ref[...])
    a = h * lax.rsqrt(_head_sumsq(h) * (1.0 / HEAD_DIM) + NORM_EPS) * gain_ref[...] * _silu(mz)

    b = b_ref[...].astype(F32) * _silu(gz)

    n_groups = len(DIL_PATTERNS)
    for g, (o_ref, l_ref, (_, d)) in enumerate(zip((o0_ref, o1_ref, o2_ref), (l0_ref, l1_ref, l2_ref), DIL_PATTERNS)):
        for r in range(d):
            dst = slice(None) if d == 1 else pl.ds(r, tm // d, stride=d)
            nat_ref[g, dst, :] = o_ref[0, r].astype(F32)
            nat_ref[n_groups + g, dst, :] = l_ref[0, r]
    outs = [nat_ref[g] for g in range(n_groups)]
    lses = [nat_ref[n_groups + g] for g in range(n_groups)]
    mx = jnp.maximum(jnp.maximum(lses[0], lses[1]), lses[2])
    es = [jnp.exp(l - mx) for l in lses]
    inv = 1.0 / (es[0] + es[1] + es[2])
    c = jnp.concatenate([outs[g] * (es[g] * inv) for g in range(3)], axis=1) * _silu(dz)

    w = w_ref[...]
    y = jnp.dot(a.astype(BF16), w[:MLSTM_W], preferred_element_type=F32)
    y += jnp.dot(b.astype(BF16), w[MLSTM_W:MLSTM_W + GQA_W], preferred_element_type=F32)
    y += jnp.dot(c.astype(BF16), w[MLSTM_W + GQA_W:], preferred_element_type=F32)
    ms = jnp.mean(y * y, axis=-1, keepdims=True)
    out_ref[...] = x_ref[...] + y * lax.rsqrt(ms + NORM_EPS) * gpost_ref[...]


def _outproj(x, hf, hb, z, b, dil_o, dil_l, gain, w, gpost, T, tm=TOKEN_TILE):
    n = x.shape[0]
    tps = T // tm
    row = lambda a: pl.BlockSpec((tm, a.shape[1]), lambda i: (i, 0))
    full = lambda a: pl.BlockSpec(a.shape, lambda i: (0, 0))
    cls = lambda a: pl.BlockSpec((1, a.shape[1], tm // a.shape[1], LANES), lambda i: (i // tps, 0, i % tps, 0))
    rows = [x, hf, hb, z, b]
    dil = list(dil_o) + list(dil_l)
    consts = [gain, w, gpost]
    return pl.pallas_call(
        functools.partial(_outproj_kernel, tm=tm),
        grid=(n // tm,),
        in_specs=[row(a) for a in rows] + [cls(a) for a in dil] + [full(a) for a in consts],
        out_specs=pl.BlockSpec((tm, D_MODEL), lambda i: (i, 0)),
        out_shape=jax.ShapeDtypeStruct((n, D_MODEL), F32),
        scratch_shapes=[pltpu.VMEM((2 * len(DIL_PATTERNS), tm, LANES), F32)],
        compiler_params=_params(("parallel",)),
        name="outproj",
    )(*rows, *dil, *consts)


def _rope_tables(T):
    rows = T // GRID_W
    r, c = jnp.meshgrid(jnp.arange(rows), jnp.arange(GRID_W), indexing="ij")
    r = r.reshape(-1).astype(F32)
    c = c.reshape(-1).astype(F32)
    axis_dim = HEAD_DIM // 2
    inv = ROPE_THETA ** (-jnp.arange(0, axis_dim, 2, dtype=F32) / axis_dim)
    ang = jnp.concatenate([r[:, None] * inv, c[:, None] * inv], axis=-1)
    cos, sin = jnp.cos(ang), jnp.sin(ang)
    cos_t = jnp.tile(jnp.concatenate([cos, cos], axis=-1), (1, LANES // HEAD_DIM))
    sin_t = jnp.tile(jnp.concatenate([-sin, sin], axis=-1), (1, LANES // HEAD_DIM))
    return cos_t, sin_t


def _split_w_in(w):
    cuts = np.cumsum((0,) + IN_SPLITS)
    mq, mk, mv, mo, mg, mz, gq, gk, gv, gz, dq, dk, dv, dz = [w[:, cuts[i]:cuts[i + 1]] for i in range(14)]
    pad = jnp.zeros((w.shape[0], GATE_PAD - mg.shape[1]), w.dtype)
    cat = lambda *xs: jnp.concatenate(xs, axis=1).astype(BF16)
    return cat(mq, mk, mv, mg, pad), cat(gq, gk, gv), cat(dq, dk, dv), cat(mo, mz, gz, dz)


def _alibi_slopes():
    return [float(np.exp2(np.float32(-8.0) * np.float32(i) / np.float32(DIL_HEADS)))
            for i in range(1, DIL_HEADS + 1)]


def _layer(x, B, T, p, cos, sin, slopes):
    n = B * T
    mb = _gqa_bound(p["q_norm"], p["k_norm"])
    m_qkv, m_gates, qt, k, vt, *dil_qkv, z = _inproj(
        x, p["norm_pre"], *p["w_in"], p["conv_w"], p["conv_b"], cos, sin, p["q_norm"], p["k_norm"],
        jnp.full((1, LANES), -mb, F32), B, T, TOKEN_TILE)
    hf, hb = _mlstm(m_qkv.reshape(B, T, 3 * MLSTM_W), m_gates.reshape(B, T, GATE_PAD), p["gate_bias"])
    b = _gqa(qt, k, vt, mb, TOKEN_TILE)

    dil_o, dil_l = [], []
    for g, (_, dilation) in enumerate(DIL_PATTERNS):
        o_g, l_g = _dilated(dil_qkv[g], dilation, slopes[g * DIL_HEADS_PER_GROUP:(g + 1) * DIL_HEADS_PER_GROUP])
        dil_o.append(o_g)
        dil_l.append(l_g)

    return _outproj(x, hf.reshape(n, MLSTM_W), hb.reshape(n, MLSTM_W), z, b.reshape(n, GQA_W),
                    dil_o, dil_l, p["out_gain"], p["w_out"], p["norm_post"], T)


def _trunk(x, layers):
    B, T, _ = x.shape
    cos, sin = _rope_tables(T)
    slopes = _alibi_slopes()
    x = x.reshape(B * T, D_MODEL)
    for p in layers:
        x = _layer(x, B, T, p, cos, sin, slopes)
    return x.reshape(B, T, D_MODEL)


def kernel(x_prompt, x_sample, norm_pre, w_in, mlstm_gate_bias, mlstm_conv_w, mlstm_conv_b, mlstm_out_gain,
           gqa_q_norm, gqa_k_norm, w_out, norm_post):
    depth = w_in.shape[0]
    row = lambda a: a.reshape(1, -1).astype(F32)
    layers = []
    for l in range(depth):
        gate_bias = jnp.concatenate(
            [mlstm_gate_bias[l].astype(F32), jnp.zeros((GATE_PAD - 4 * MLSTM_HEADS,), F32)]).reshape(1, GATE_PAD)
        layers.append(dict(
            norm_pre=row(norm_pre[l]),
            w_in=_split_w_in(w_in[l]),
            gate_bias=gate_bias,
            conv_w=mlstm_conv_w[l].astype(F32),
            conv_b=row(mlstm_conv_b[l]),
            out_gain=row(mlstm_out_gain[l]),
            q_norm=row(jnp.tile(gqa_q_norm[l], GQA_Q_HEADS)),
            k_norm=row(jnp.tile(gqa_k_norm[l], GQA_KV_HEADS)),
            w_out=w_out[l].astype(BF16),
            norm_post=row(norm_post[l]),
        ))
    return (_trunk(x_prompt, layers), _trunk(x_sample, layers))
```
